```python
import math
import jax, jax.numpy as jnp
from jax import lax
import numpy as np

D_MODEL = 1024
BATCH = 2
SEQ = 16384
DEPTH = 4

N_EVEN = (DEPTH + 1) // 2
N_ODD = DEPTH // 2
RMS_EPS = 1e-6
D_FF = -(-8 * D_MODEL // (3 * 256)) * 256
MIX_WIDTH = D_MODEL

POOL_WIDTH = MIX_WIDTH // 4
POOL_WINDOWS = (2, 4, 8, 16)
POOL_GROUPS = len(POOL_WINDOWS)
POOL_GROUP_DIM = POOL_WIDTH // POOL_GROUPS
SSD_WIDTH = MIX_WIDTH - POOL_WIDTH
SSD_HEAD_DIM = 64
SSD_HEADS = SSD_WIDTH // SSD_HEAD_DIM
SSD_GROUPS = 2
SSD_HEADS_PER_GROUP = SSD_HEADS // SSD_GROUPS
SSD_STATE = 128
SSD_CONV = 4
SSD_CHUNK = 128
SSD_BC_DIM = SSD_GROUPS * SSD_STATE
SSD_CONV_DIM = SSD_WIDTH + 2 * SSD_BC_DIM
EVEN_IN = POOL_WIDTH + SSD_WIDTH + SSD_CONV_DIM + SSD_HEADS

RWKV_WIDTH = MIX_WIDTH // 2
RWKV_HEAD_DIM = 64
RWKV_HEADS = RWKV_WIDTH // RWKV_HEAD_DIM
RWKV_DECAY_RANK = 64
RWKV_ICLR_RANK = 64
RWKV_GATE_RANK = 128
RWKV_IN = 3 * RWKV_WIDTH + RWKV_DECAY_RANK + RWKV_ICLR_RANK + RWKV_GATE_RANK
RWKV_GN_EPS = 64e-5
RWKV_DECAY_OFFSET = 0.5
LRU_WIDTH = MIX_WIDTH - RWKV_WIDTH
LRU_BLOCKS = 8
LRU_BLOCK_DIM = LRU_WIDTH // LRU_BLOCKS
LRU_CONV = 4
LRU_C = 8.0
ODD_IN = RWKV_IN + 2 * LRU_WIDTH

kernel_name = "hybrid_pool_ssd_rwkv7_rglru_trunk"


def rmsnorm(x, g):
    xf = x.astype(jnp.float32)
    y = xf * lax.rsqrt(jnp.mean(xf * xf, axis=-1, keepdims=True) + RMS_EPS)
    return (y * g.astype(jnp.float32)).astype(x.dtype)


def causal_depthwise_conv(x, w, b):
    k = w.shape[0]
    y = lax.conv_general_dilated(x, w[:, None, :], window_strides=(1,), padding=((k - 1, 0),),
                                 dimension_numbers=("NWC", "WIO", "NWC"),
                                 feature_group_count=x.shape[-1])
    return y + b


def swiglu(h, w_gate, w_up, w_down):
    return (jax.nn.silu(h @ w_gate) * (h @ w_up)) @ w_down


def pool_mixer(u, pool_w, pool_scale):
    bsz, s, _ = u.shape
    uf = u.astype(jnp.float32).reshape(bsz, s, POOL_GROUPS, POOL_GROUP_DIM)
    cs = jnp.cumsum(uf, axis=1)
    cs = jnp.concatenate([jnp.zeros_like(cs[:, :1]), cs], axis=1)
    pos = jnp.arange(1, s + 1, dtype=jnp.float32)
    pooled = []
    for gi, w in enumerate(POOL_WINDOWS):
        c = cs[:, :, gi]
        lo = jnp.concatenate([jnp.zeros_like(c[:, :w - 1]), c[:, :s + 1 - w]], axis=1)
        cnt = jnp.minimum(pos, float(w))[None, :, None]
        pooled.append((c[:, 1:] - lo) / cnt)
    pooled = jnp.stack(pooled, axis=2)
    d = (pooled - uf).astype(u.dtype)
    y = jnp.einsum("bsgc,gcd->bsgd", d, pool_w).reshape(bsz, s, POOL_WIDTH)
    return y * pool_scale


def ssd_chunked_scan(x, dt, a, bm, cm):
    bsz, s = x.shape[:2]
    nc = s // SSD_CHUNK
    L = SSD_CHUNK
    xc = (x * dt[..., None]).reshape(bsz, nc, L, SSD_GROUPS, SSD_HEADS_PER_GROUP, SSD_HEAD_DIM)
    bc = bm.reshape(bsz, nc, L, SSD_GROUPS, SSD_STATE)
    cc = cm.reshape(bsz, nc, L, SSD_GROUPS, SSD_STATE)
    da = jnp.transpose((dt * a).reshape(bsz, nc, L, SSD_GROUPS, SSD_HEADS_PER_GROUP), (0, 1, 3, 4, 2))
    cum = jnp.cumsum(da, axis=-1)
    seg = cum[..., :, None] - cum[..., None, :]
    mask = jnp.tril(jnp.ones((L, L), dtype=bool))
    decay = jnp.where(mask, jnp.exp(jnp.minimum(seg, 0.0)), 0.0)
    cb = jnp.einsum("bclgn,bcsgn->bcgls", cc, bc)
    y_diag = jnp.einsum("bcgls,bcgels,bcsgep->bclgep", cb, decay, xc)
    decay_to_end = jnp.exp(cum[..., -1:] - cum)
    chunk_states = jnp.einsum("bclgn,bcgel,bclgep->bcgepn", bc, decay_to_end, xc)
    chunk_decay = jnp.exp(cum[..., -1])

    def step(h, inp):
        st, dec = inp
        return h * dec[..., None, None] + st, h

    h0 = jnp.zeros((bsz, SSD_GROUPS, SSD_HEADS_PER_GROUP, SSD_HEAD_DIM, SSD_STATE), jnp.float32)
    _, prev = lax.scan(step, h0, (jnp.moveaxis(chunk_states, 1, 0), jnp.moveaxis(chunk_decay, 1, 0)))
    prev = jnp.moveaxis(prev, 0, 1)
    y_off = jnp.einsum("bclgn,bcgepn,bcgel->bclgep", cc, prev, jnp.exp(cum))
    return (y_diag + y_off).reshape(bsz, s, SSD_GROUPS, SSD_HEADS_PER_GROUP, SSD_HEAD_DIM)


def ssd_mixer(z, xbc, dt_raw, conv_w, conv_b, dt_bias, a_log, d_skip, norm_g):
    bsz, s, _ = z.shape
    f32 = jnp.float32
    xbc = jax.nn.silu(causal_depthwise_conv(xbc, conv_w, conv_b)).astype(f32)
    xh = xbc[..., :SSD_WIDTH].reshape(bsz, s, SSD_GROUPS, SSD_HEADS_PER_GROUP, SSD_HEAD_DIM)
    bm = xbc[..., SSD_WIDTH:SSD_WIDTH + SSD_BC_DIM].reshape(bsz, s, SSD_GROUPS, SSD_STATE)
    cm = xbc[..., SSD_WIDTH + SSD_BC_DIM:].reshape(bsz, s, SSD_GROUPS, SSD_STATE)
    dt = jax.nn.softplus(dt_raw.astype(f32) + dt_bias.astype(f32)).reshape(bsz, s, SSD_GROUPS, SSD_HEADS_PER_GROUP)
    a = -jnp.exp(a_log.astype(f32)).reshape(SSD_GROUPS, SSD_HEADS_PER_GROUP)
    y = ssd_chunked_scan(xh, dt, a, bm, cm)
    y = y + d_skip.astype(f32).reshape(SSD_GROUPS, SSD_HEADS_PER_GROUP)[:, :, None] * xh
    gsz = SSD_WIDTH // SSD_GROUPS
    y = y.reshape(bsz, s, SSD_GROUPS, gsz) * jax.nn.silu(z.astype(f32)).reshape(bsz, s, SSD_GROUPS, gsz)
    y = y * lax.rsqrt(jnp.mean(y * y, axis=-1, keepdims=True) + RMS_EPS)
    return (y.reshape(bsz, s, SSD_WIDTH) * norm_g.astype(f32)).astype(z.dtype)


def token_shift(p, mu):
    prev = jnp.concatenate([jnp.zeros_like(p[:, :1]), p[:, :-1]], axis=1)
    return p + mu * (prev - p)


def rwkv7_scan(r, w, k, v, kk, a):
    def step(st, inp):
        r_t, w_t, k_t, v_t, kk_t, a_t = inp
        sa = jnp.einsum("bhvk,bhk->bhv", st, -kk_t)
        st = (st * w_t[:, :, None, :] + sa[..., None] * (kk_t * a_t)[:, :, None, :]
              + v_t[..., None] * k_t[:, :, None, :])
        return st, jnp.einsum("bhvk,bhk->bhv", st, r_t)

    bsz = r.shape[0]
    s0 = jnp.zeros((bsz, RWKV_HEADS, RWKV_HEAD_DIM, RWKV_HEAD_DIM), jnp.float32)
    xs = tuple(jnp.moveaxis(t, 1, 0) for t in (r, w, k, v, kk, a))
    _, y = lax.scan(step, s0, xs)
    return jnp.moveaxis(y, 0, 1)


def rwkv7_mixer(p, mu, w0, w_up, a0, a_up, g_up, k_k, k_a, r_k, ln_g, ln_b):
    bsz, s, _ = p.shape
    f32 = jnp.float32
    p = token_shift(p, mu)
    o1, o2, o3 = RWKV_WIDTH, 2 * RWKV_WIDTH, 3 * RWKV_WIDTH
    o4, o5 = o3 + RWKV_DECAY_RANK, o3 + RWKV_DECAY_RANK + RWKV_ICLR_RANK
    r, k, v = p[..., :o1], p[..., o1:o2], p[..., o2:o3]
    wd, ad, gd = p[..., o3:o4], p[..., o4:o5], p[..., o5:]
    w_log = -jax.nn.softplus(-(w0 + jnp.tanh(wd) @ w_up).astype(f32)) - RWKV_DECAY_OFFSET
    decay = jnp.exp(-jnp.exp(w_log))
    a = jax.nn.sigmoid((a0 + ad @ a_up).astype(f32))
    g = (jax.nn.sigmoid(gd) @ g_up).astype(f32)
    heads = lambda t: t.astype(f32).reshape(bsz, s, RWKV_HEADS, RWKV_HEAD_DIM)
    r, k, v, decay, a = heads(r), heads(k), heads(v), heads(decay), heads(a)
    kk = k * k_k.astype(f32).reshape(RWKV_HEADS, RWKV_HEAD_DIM)
    kk = kk * lax.rsqrt(jnp.sum(kk * kk, axis=-1, keepdims=True) + 1e-12)
    k = k * (1.0 + (a - 1.0) * k_a.astype(f32).reshape(RWKV_HEADS, RWKV_HEAD_DIM))
    y = rwkv7_scan(r, decay, k, v, kk, a)
    mean = jnp.mean(y, axis=-1, keepdims=True)
    var = jnp.mean(jnp.square(y - mean), axis=-1, keepdims=True)
    y = ((y - mean) * lax.rsqrt(var + RWKV_GN_EPS)).reshape(bsz, s, RWKV_WIDTH)
    y = y * ln_g.astype(f32) + ln_b.astype(f32)
    bonus = jnp.sum(r * k * r_k.astype(f32), axis=-1, keepdims=True) * v
    y = y + bonus.reshape(bsz, s, RWKV_WIDTH)
    return (y * g).astype(p.dtype)


def _linear_combine(left, right):
    a_l, b_l = left
    a_r, b_r = right
    return a_l * a_r, a_r * b_l + b_r


def rglru_mixer(gate, xb, conv_w, conv_b, wa, ba, wx, bx, lam):
    bsz, s, _ = xb.shape
    f32 = jnp.float32
    xb = causal_depthwise_conv(xb, conv_w, conv_b)
    xblk = xb.reshape(bsz, s, LRU_BLOCKS, LRU_BLOCK_DIM)
    rg = jax.nn.sigmoid((jnp.einsum("bshi,hij->bshj", xblk, wa).reshape(bsz, s, LRU_WIDTH) + ba).astype(f32))
    ig = jax.nn.sigmoid((jnp.einsum("bshi,hij->bshj", xblk, wx).reshape(bsz, s, LRU_WIDTH) + bx).astype(f32))
    log_a = -LRU_C * rg * jax.nn.softplus(-lam.astype(f32))
    a = jnp.exp(log_a)
    mult = jnp.sqrt(-jnp.expm1(2.0 * log_a))
    b = mult * ig * xb.astype(f32)
    _, h = lax.associative_scan(_linear_combine, (a, b), axis=1)
    return (h * jax.nn.gelu(gate.astype(f32))).astype(xb.dtype)


def even_mixer(h, w_in, w_out, pool_w, pool_scale, conv_w, conv_b, dt_bias, a_log, d_skip, norm_g):
    p = h @ w_in
    o1 = POOL_WIDTH
    o2 = o1 + SSD_WIDTH
    o3 = o2 + SSD_CONV_DIM
    y_pool = pool_mixer(p[..., :o1], pool_w, pool_scale)
    y_ssd = ssd_mixer(p[..., o1:o2], p[..., o2:o3], p[..., o3:], conv_w, conv_b, dt_bias, a_log, d_skip, norm_g)
    return jnp.concatenate([y_pool, y_ssd], axis=-1) @ w_out


def odd_mixer(h, w_in, w_out, mu, w0, w_up, a0, a_up, g_up, k_k, k_a, r_k, ln_g, ln_b,
              lconv_w, lconv_b, wa, ba, wx, bx, lam):
    p = h @ w_in
    y_rwkv = rwkv7_mixer(p[..., :RWKV_IN], mu, w0, w_up, a0, a_up, g_up, k_k, k_a, r_k, ln_g, ln_b)
    y_lru = rglru_mixer(p[..., RWKV_IN:RWKV_IN + LRU_WIDTH], p[..., RWKV_IN + LRU_WIDTH:],
                        lconv_w, lconv_b, wa, ba, wx, bx, lam)
    return jnp.concatenate([y_rwkv, y_lru], axis=-1) @ w_out


def setup_inputs(seed: int = 0) -> dict:
    key = jax.random.key(seed)
    ks = jax.random.split(key, 48)
    f32 = jnp.float32
    nrm = lambda i, shape, scale: scale * jax.random.normal(ks[i], shape, f32)
    unif = lambda i, shape, lo, hi: jax.random.uniform(ks[i], shape, f32, lo, hi)
    E, O = N_EVEN, N_ODD
    dt0 = jnp.exp(unif(16, (E, SSD_HEADS), math.log(1e-3), math.log(1e-1)))
    lru_a = unif(40, (O, LRU_WIDTH), 0.9, 0.999) ** (1.0 / LRU_C)
    return {
        "x": nrm(0, (BATCH, SEQ, D_MODEL), 1.0),
        "mix_norm_g": 1.0 + nrm(1, (DEPTH, D_MODEL), 0.05),
        "ffn_norm_g": 1.0 + nrm(2, (DEPTH, D_MODEL), 0.05),
        "ffn_w_gate": nrm(3, (DEPTH, D_MODEL, D_FF), D_MODEL ** -0.5),
        "ffn_w_up": nrm(4, (DEPTH, D_MODEL, D_FF), D_MODEL ** -0.5),
        "ffn_w_down": nrm(5, (DEPTH, D_FF, D_MODEL), 0.5 * D_FF ** -0.5),
        "final_norm_g": 1.0 + nrm(6, (D_MODEL,), 0.05),
        "ev_w_in": nrm(7, (E, D_MODEL, EVEN_IN), D_MODEL ** -0.5),
        "ev_w_out": nrm(8, (E, MIX_WIDTH, D_MODEL), 0.5 * MIX_WIDTH ** -0.5),
        "pool_w": nrm(9, (E, POOL_GROUPS, POOL_GROUP_DIM, POOL_GROUP_DIM), POOL_GROUP_DIM ** -0.5),
        "pool_scale": 1.0 + nrm(10, (E, POOL_WIDTH), 0.05),
        "ssd_conv_w": nrm(11, (E, SSD_CONV, SSD_CONV_DIM), 0.5),
        "ssd_conv_b": nrm(12, (E, SSD_CONV_DIM), 0.02),
        "ssd_dt_bias": dt0 + jnp.log(-jnp.expm1(-dt0)),
        "ssd_a_log": jnp.log(unif(13, (E, SSD_HEADS), 1.0, 16.0)),
        "ssd_d": 1.0 + nrm(14, (E, SSD_HEADS), 0.1),
        "ssd_norm_g": 1.0 + nrm(15, (E, SSD_WIDTH), 0.05),
        "od_w_in": nrm(20, (O, D_MODEL, ODD_IN), D_MODEL ** -0.5),
        "od_w_out": nrm(21, (O, MIX_WIDTH, D_MODEL), 0.5 * MIX_WIDTH ** -0.5),
        "rwkv_mu": unif(22, (O, RWKV_IN), 0.0, 1.0),
        "rwkv_w0": unif(23, (O, RWKV_WIDTH), -5.0, 1.0),
        "rwkv_w_up": nrm(24, (O, RWKV_DECAY_RANK, RWKV_WIDTH), 0.1),
        "rwkv_a0": nrm(25, (O, RWKV_WIDTH), 0.5),
        "rwkv_a_up": nrm(26, (O, RWKV_ICLR_RANK, RWKV_WIDTH), 0.1),
        "rwkv_g_up": nrm(27, (O, RWKV_GATE_RANK, RWKV_WIDTH), RWKV_GATE_RANK ** -0.5),
        "rwkv_k_k": 0.85 + nrm(28, (O, RWKV_WIDTH), 0.05),
        "rwkv_k_a": 1.0 + nrm(29, (O, RWKV_WIDTH), 0.05),
        "rwkv_r_k": nrm(30, (O, RWKV_HEADS, RWKV_HEAD_DIM), 0.1),
        "rwkv_ln_g": 1.0 + nrm(31, (O, RWKV_WIDTH), 0.05),
        "rwkv_ln_b": nrm(32, (O, RWKV_WIDTH), 0.02),
        "lru_conv_w": nrm(33, (O, LRU_CONV, LRU_WIDTH), 0.5),
        "lru_conv_b": nrm(34, (O, LRU_WIDTH), 0.02),
        "lru_wa": nrm(35, (O, LRU_BLOCKS, LRU_BLOCK_DIM, LRU_BLOCK_DIM), LRU_BLOCK_DIM ** -0.5),
        "lru_ba": nrm(36, (O, LRU_WIDTH), 0.02),
        "lru_wx": nrm(37, (O, LRU_BLOCKS, LRU_BLOCK_DIM, LRU_BLOCK_DIM), LRU_BLOCK_DIM ** -0.5),
        "lru_bx": nrm(38, (O, LRU_WIDTH), 0.02),
        "lru_lambda": jnp.log(lru_a) - jnp.log1p(-lru_a),
    }


def reference(x, mix_norm_g, ffn_norm_g, ffn_w_gate, ffn_w_up, ffn_w_down, final_norm_g,
              ev_w_in, ev_w_out, pool_w, pool_scale, ssd_conv_w, ssd_conv_b, ssd_dt_bias, ssd_a_log,
              ssd_d, ssd_norm_g, od_w_in, od_w_out, rwkv_mu, rwkv_w0, rwkv_w_up, rwkv_a0, rwkv_a_up,
              rwkv_g_up, rwkv_k_k, rwkv_k_a, rwkv_r_k, rwkv_ln_g, rwkv_ln_b, lru_conv_w, lru_conv_b,
              lru_wa, lru_ba, lru_wx, lru_bx, lru_lambda):
    h = x
    for layer in range(DEPTH):
        i = layer // 2
        hn = rmsnorm(h, mix_norm_g[layer])
        if layer % 2 == 0:
            mix = even_mixer(hn, ev_w_in[i], ev_w_out[i], pool_w[i], pool_scale[i], ssd_conv_w[i],
                             ssd_conv_b[i], ssd_dt_bias[i], ssd_a_log[i], ssd_d[i], ssd_norm_g[i])
        else:
            mix = odd_mixer(hn, od_w_in[i], od_w_out[i], rwkv_mu[i], rwkv_w0[i], rwkv_w_up[i], rwkv_a0[i],
                            rwkv_a_up[i], rwkv_g_up[i], rwkv_k_k[i], rwkv_k_a[i], rwkv_r_k[i], rwkv_ln_g[i],
                            rwkv_ln_b[i], lru_conv_w[i], lru_conv_b[i], lru_wa[i], lru_ba[i], lru_wx[i],
                            lru_bx[i], lru_lambda[i])
        h = h + mix
        hn = rmsnorm(h, ffn_norm_g[layer])
        h = h + swiglu(hn, ffn_w_gate[layer], ffn_w_up[layer], ffn_w_down[layer])
    return rmsnorm(h, final_norm_g)
```

```python
import functools

import jax
import jax.numpy as jnp
from jax import lax
from jax.experimental import pallas as pl
from jax.experimental.pallas import tpu as pltpu

F32 = jnp.float32
BF16 = jnp.bfloat16

D_MODEL = 1024
D_FF = 2816
RMS_EPS = 1e-6

POOL_WIDTH = 256
POOL_WINDOWS = (2, 4, 8, 16)
POOL_GROUP_DIM = 64
POOL_HALO = 16

SSD_WIDTH = 768
SSD_HEADS = 12
SSD_HEAD_DIM = 64
SSD_GROUPS = 2
SSD_GROUP_WIDTH = SSD_WIDTH // SSD_GROUPS
SSD_STATE = 128
SSD_CHUNK = 128
SSD_CONV_DIM = 1280
CONV_K = 4
CONV_HALO = 8

RWKV_WIDTH = 512
RWKV_HEAD_DIM = 64
RWKV_PAIRS = RWKV_WIDTH // 128
RWKV_CHUNK = 64
RWKV_LORA_PAD = 128
RWKV_SHIFT_WIDTH = 3 * RWKV_WIDTH + 3 * RWKV_LORA_PAD
RWKV_GN_EPS = 64e-5
RWKV_DECAY_OFFSET = 0.5
LRU_WIDTH = 512
LRU_C = 8.0

EVEN_IN_PAD = POOL_WIDTH + SSD_WIDTH + SSD_CONV_DIM + SSD_WIDTH
ODD_IN_PAD = RWKV_SHIFT_WIDTH + 2 * LRU_WIDTH

SEQ_TILE = 512
FFN_ROWS = 512
VMEM_LIMIT = 56 * 1024 * 1024


def _mm(a, b):
    return jnp.dot(a.astype(BF16), b.astype(BF16), preferred_element_type=F32)


def _mm_nt(a, b):
    return lax.dot_general(a.astype(BF16), b.astype(BF16), (((1,), (1,)), ((), ())),
                           preferred_element_type=F32)


def _mm_tn(a, b):
    return lax.dot_general(a.astype(BF16), b.astype(BF16), (((0,), (0,)), ((), ())),
                           preferred_element_type=F32)


def _dot_f32(a, b, ca, cb):
    return lax.dot_general(a, b, (((ca,), (cb,)), ((), ())), precision=lax.Precision.HIGHEST,
                           preferred_element_type=F32)


def _mm_f32(a, b):
    return _dot_f32(a, b, 1, 0)


def _mm_f32_nt(a, b):
    return _dot_f32(a, b, 1, 1)


def _mm_f32_tn(a, b):
    return _dot_f32(a, b, 0, 0)


def _rmsnorm(x, g):
    ms = jnp.mean(x * x, axis=-1, keepdims=True)
    return x * lax.rsqrt(ms + RMS_EPS) * g


def _silu(x):
    return x * jax.nn.sigmoid(x)


def _iota2(shape, axis):
    return lax.broadcasted_iota(jnp.int32, shape, axis)


def _causal_conv(x, prev_ref, w, b, rows):
    xe = jnp.concatenate([prev_ref[...], x], axis=0)
    prev_ref[...] = x[rows - CONV_HALO:, :]
    acc = xe * w[CONV_K - 1:CONV_K, :]
    for j in range(1, CONV_K):
        acc = acc + pltpu.roll(xe, j, 0) * w[CONV_K - 1 - j:CONV_K - j, :]
    return acc[CONV_HALO:, :] + b


def _ffn_body(h_ref, g_ref, wg_ref, wu_ref, wd_ref, fg_ref, o_ref, *, final):
    h = h_ref[...]
    hn = _rmsnorm(h, g_ref[...]).astype(BF16)
    gate = jnp.dot(hn, wg_ref[...], preferred_element_type=F32)
    up = jnp.dot(hn, wu_ref[...], preferred_element_type=F32)
    act = (_silu(gate) * up).astype(BF16)
    out = h + jnp.dot(act, wd_ref[...], preferred_element_type=F32)
    if final:
        out = _rmsnorm(out, fg_ref[...])
    o_ref[...] = out


def _resident(shape):
    nd = len(shape)
    return pl.BlockSpec(shape, lambda *_: (0,) * nd, pipeline_mode=pl.Buffered(1))


def _ffn(h2d, g, wg, wu, wd, fg, final):
    t = h2d.shape[0]
    rows = min(FFN_ROWS, t)
    row_spec = pl.BlockSpec((rows, D_MODEL), lambda i: (i, 0))
    return pl.pallas_call(
        functools.partial(_ffn_body, final=final),
        grid=(t // rows,),
        in_specs=[row_spec, _resident((1, D_MODEL)), _resident((D_MODEL, D_FF)), _resident((D_MODEL, D_FF)),
                  _resident((D_FF, D_MODEL)), _resident((1, D_MODEL))],
        out_specs=row_spec,
        out_shape=jax.ShapeDtypeStruct((t, D_MODEL), F32),
        compiler_params=pltpu.CompilerParams(dimension_semantics=("arbitrary",), vmem_limit_bytes=VMEM_LIMIT),
        name="ffn_final" if final else "ffn",
    )(h2d, g, wg, wu, wd, fg)


def _pool_mixer(u, prev_ref, poolw_bd, pscale, first_pos, rows):
    ue = jnp.concatenate([prev_ref[...], u], axis=0)
    prev_ref[...] = u[rows - POOL_HALO:, :]
    sums = []
    acc = ue
    shift = 1
    for _ in POOL_WINDOWS:
        acc = acc + pltpu.roll(acc, shift, 0)
        sums.append(acc[POOL_HALO:, :])
        shift *= 2
    pos = (first_pos + 1 + _iota2((rows, 1), 0)).astype(F32)
    lane = _iota2((1, POOL_WIDTH), 1)
    pooled = sums[-1] * (1.0 / jnp.minimum(pos, float(POOL_WINDOWS[-1])))
    for gi in range(len(POOL_WINDOWS) - 2, -1, -1):
        cand = sums[gi] * (1.0 / jnp.minimum(pos, float(POOL_WINDOWS[gi])))
        pooled = jnp.where(lane < (gi + 1) * POOL_GROUP_DIM, cand, pooled)
    return _mm(pooled - u, poolw_bd) * pscale


def _ssd_chunk(c, refs, consts):
    xc_s, da_s, b_s, c_s, y_s, state_ref = refs
    tri, causal, lane_lo = consts
    L = SSD_CHUNK
    rows = pl.ds(pl.multiple_of(c * L, L), L)
    cum = _mm_f32(tri, da_s[rows, :])
    xc = xc_s[rows, :]
    bmat = b_s[rows, :]
    cmat = c_s[rows, :]
    last = cum[L - 1:L, :]
    ecum = jnp.exp(cum)
    to_end = jnp.exp(last - cum)
    chunk_decay = jnp.exp(last)
    for g in range(SSD_GROUPS):
        gs = slice(g * SSD_GROUP_WIDTH, (g + 1) * SSD_GROUP_WIDTH)
        ns = slice(g * SSD_STATE, (g + 1) * SSD_STATE)
        bg, cg = bmat[:, ns], cmat[:, ns]
        cb = _mm_nt(cg, bg)
        hprev = state_ref[g]
        y_g = _mm(cg, hprev) * ecum[:, gs]
        ypairs = []
        for k in range(SSD_GROUP_WIDTH // 128):
            ps = slice(g * SSD_GROUP_WIDTH + k * 128, g * SSD_GROUP_WIDTH + (k + 1) * 128)
            slab = cum[:, ps]
            swapped = pltpu.roll(slab, 64, 1)
            xpair = xc[:, ps]
            outs = []
            for col in (jnp.where(lane_lo, slab, swapped), jnp.where(lane_lo, swapped, slab)):
                seg = col - col.T
                decay = jnp.where(causal, jnp.exp(jnp.minimum(seg, 0.0)), 0.0)
                outs.append(_mm(cb * decay, xpair))
            ypairs.append(jnp.where(lane_lo, outs[0], outs[1]))
        y_s[rows, gs] = y_g + jnp.concatenate(ypairs, axis=1)
        state_ref[g] = hprev * chunk_decay[:, gs] + _mm_tn(bg, xc[:, gs] * to_end[:, gs])
    return 0


def _even_body(h_ref, ng_ref, win_ref, wout_ref, poolw_ref, pscale_ref, convw_ref, convb_ref, dtb_ref, alog_ref,
               dskip_ref, ssdg_ref, o_ref, upool_prev, xbc_prev, state_ref, xc_s, da_s, b_s, c_s, y_s, *, rows):
    s = pl.program_id(1)

    @pl.when(s == 0)
    def _():
        upool_prev[...] = jnp.zeros_like(upool_prev)
        xbc_prev[...] = jnp.zeros_like(xbc_prev)
        state_ref[...] = jnp.zeros_like(state_ref)

    h = h_ref[...]
    p = _mm(_rmsnorm(h, ng_ref[...]), win_ref[...])
    o1, o2, o3 = POOL_WIDTH, POOL_WIDTH + SSD_WIDTH, POOL_WIDTH + SSD_WIDTH + SSD_CONV_DIM

    y_pool = _pool_mixer(p[:, :o1], upool_prev, poolw_ref[...], pscale_ref[...], s * rows, rows)

    z = p[:, o1:o2]
    xbc = _silu(_causal_conv(p[:, o2:o3], xbc_prev, convw_ref[...], convb_ref[...], rows))
    xh = xbc[:, :SSD_WIDTH]
    dt = jax.nn.softplus(p[:, o3:] + dtb_ref[...])
    xc_s[...] = xh * dt
    da_s[...] = dt * (-jnp.exp(alog_ref[...]))
    b_s[...] = xbc[:, SSD_WIDTH:SSD_WIDTH + SSD_GROUPS * SSD_STATE]
    c_s[...] = xbc[:, SSD_WIDTH + SSD_GROUPS * SSD_STATE:]

    L = SSD_CHUNK
    r_i, c_i = _iota2((L, L), 0), _iota2((L, L), 1)
    consts = ((r_i >= c_i).astype(F32), r_i >= c_i, _iota2((1, 128), 1) < 64)
    refs = (xc_s, da_s, b_s, c_s, y_s, state_ref)
    lax.fori_loop(0, rows // L, lambda c, carry: _ssd_chunk(c, refs, consts), 0)

    y = (y_s[...] + dskip_ref[...] * xh) * _silu(z)
    normed = []
    for g in range(SSD_GROUPS):
        yy = y[:, g * SSD_GROUP_WIDTH:(g + 1) * SSD_GROUP_WIDTH]
        normed.append(yy * lax.rsqrt(jnp.mean(yy * yy, axis=-1, keepdims=True) + RMS_EPS))
    y_ssd = jnp.concatenate(normed, axis=1) * ssdg_ref[...]

    wout = wout_ref[...]
    o_ref[...] = h + _mm(y_pool, wout[:POOL_WIDTH, :]) + _mm(y_ssd, wout[POOL_WIDTH:, :])


def _seq_call(body, name, h, consts, scratch, rows):
    bsz, seq, _ = h.shape
    tile = pl.BlockSpec((pl.Squeezed(), rows, D_MODEL), lambda b, s: (b, s, 0))
    return pl.pallas_call(
        functools.partial(body, rows=rows),
        grid=(bsz, seq // rows),
        in_specs=[tile] + [_resident(c.shape) for c in consts],
        out_specs=tile,
        out_shape=jax.ShapeDtypeStruct(h.shape, F32),
        scratch_shapes=scratch,
        compiler_params=pltpu.CompilerParams(dimension_semantics=("arbitrary", "arbitrary"),
                                             vmem_limit_bytes=VMEM_LIMIT),
        name=name,
    )(h, *consts)


def _block_diag(w):
    g, c, _ = w.shape
    eye = jnp.eye(g, dtype=w.dtype)
    return (eye[:, None, :, None] * w[:, :, None, :]).reshape(g * c, g * c)


def _row(v):
    return v.reshape(1, -1).astype(F32)


def _even_layer(h, ng, w_in, w_out, pool_w, pool_scale, conv_w, conv_b, dt_bias, a_log, d_skip, norm_g, rows):
    o3 = POOL_WIDTH + SSD_WIDTH + SSD_CONV_DIM
    rep = lambda v: jnp.repeat(v, SSD_HEAD_DIM, axis=-1)
    w_in_p = jnp.concatenate([w_in[:, :o3], rep(w_in[:, o3:])], axis=1).astype(BF16)
    consts = [_row(ng), w_in_p, w_out.astype(BF16), _block_diag(pool_w).astype(BF16), _row(pool_scale),
              conv_w.astype(F32), _row(conv_b), _row(rep(dt_bias)), _row(rep(a_log)), _row(rep(d_skip)),
              _row(norm_g)]
    scratch = [pltpu.VMEM((POOL_HALO, POOL_WIDTH), F32), pltpu.VMEM((CONV_HALO, SSD_CONV_DIM), F32),
               pltpu.VMEM((SSD_GROUPS, SSD_STATE, SSD_GROUP_WIDTH), F32),
               pltpu.VMEM((rows, SSD_WIDTH), F32), pltpu.VMEM((rows, SSD_WIDTH), F32),
               pltpu.VMEM((rows, SSD_GROUPS * SSD_STATE), F32), pltpu.VMEM((rows, SSD_GROUPS * SSD_STATE), F32),
               pltpu.VMEM((rows, SSD_WIDTH), F32)]
    return _seq_call(_even_body, "even_mixer", h, consts, scratch, rows)


def _stack_heads(x, lane_lo):
    return jnp.concatenate([jnp.where(lane_lo, x, 0.0), jnp.where(lane_lo, 0.0, x)], axis=0)


def _unit_lower_inverse(n, level_masks):
    size = n.shape[0]
    eye = (_iota2((size, size), 0) == _iota2((size, size), 1)).astype(F32)
    x = eye + jnp.where(level_masks[0], n, 0.0)
    for m in level_masks[1:]:
        x = x + _mm_f32(x, _mm_f32(jnp.where(m, n, 0.0), x))
    return x


def _rwkv_chunk(c, refs, consts):
    r_s, k_s, v_s, kk_s, a_s, ld_s, y_s, state_ref = refs
    tri, strict, incl, lane_lo, level_masks = consts
    L = RWKV_CHUNK
    rows = pl.ds(pl.multiple_of(c * L, L), L)
    ld = ld_s[rows, :]
    cum = _mm_f32(tri, ld)
    last = cum[L - 1:L, :]
    e_pos, e_neg, e_end = jnp.exp(cum), jnp.exp(-cum), jnp.exp(last - cum)
    kk = kk_s[rows, :]
    kka = kk * a_s[rows, :]
    k = k_s[rows, :]
    r_t = r_s[rows, :] * e_pos
    a_t = -kk * jnp.exp(cum - ld)
    b_t, k_t = kka * e_neg, k * e_neg
    b_e, k_e = kka * e_end, k * e_end
    v = v_s[rows, :]
    chunk_decay = jnp.exp(last)
    for q in range(RWKV_PAIRS):
        sl = slice(q * 128, (q + 1) * 128)
        st = lambda x: _stack_heads(x[:, sl], lane_lo)
        a2, r2, b2, k2, v2 = st(a_t), st(r_t), st(b_t), st(k_t), st(v)
        n_ab = jnp.where(strict, _mm_f32_nt(a2, b2), 0.0)
        n_ak = jnp.where(strict, _mm_f32_nt(a2, k2), 0.0)
        n_rb = jnp.where(incl, _mm_f32_nt(r2, b2), 0.0)
        n_rk = jnp.where(incl, _mm_f32_nt(r2, k2), 0.0)
        state = state_ref[q]
        w2 = _mm_f32_nt(a2, state) + _mm_f32(n_ak, v2)
        u2 = _mm_f32(_unit_lower_inverse(n_ab, level_masks), w2)
        y2 = _mm_f32_nt(r2, state) + _mm_f32(n_rb, u2) + _mm_f32(n_rk, v2)
        y_s[rows, sl] = y2[:L, :] + y2[L:, :]
        state_ref[q] = state * chunk_decay[:, sl] + _mm_f32_tn(u2, st(b_e)) + _mm_f32_tn(v2, st(k_e))
    return 0


def _head_sum(x, ones_bd):
    return _mm_f32(x, ones_bd)


def _lru_scan(a, b, carry_ref, rows):
    row = _iota2((rows, 1), 0)
    d = 1
    while d < rows:
        valid = row >= d
        b = jnp.where(valid, a * pltpu.roll(b, d, 0) + b, b)
        a = jnp.where(valid, a * pltpu.roll(a, d, 0), a)
        d *= 2
    hseq = b + a * carry_ref[CONV_HALO - 1:CONV_HALO, :]
    carry_ref[...] = hseq[rows - CONV_HALO:, :]
    return hseq


def _odd_body(h_ref, ng_ref, win_ref, wout_ref, mu_ref, w0_ref, wup_ref, a0_ref, aup_ref, gup_ref, kk_ref, ka_ref,
              rk_ref, lng_ref, lnb_ref, lcw_ref, lcb_ref, wax_ref, bax_ref, lam_ref, ones_ref, o_ref,
              shift_prev, lru_prev, lru_carry, state_ref, r_s, k_s, v_s, kk_s, a_s, ld_s, y_s, *, rows):
    s = pl.program_id(1)

    @pl.when(s == 0)
    def _():
        shift_prev[...] = jnp.zeros_like(shift_prev)
        lru_prev[...] = jnp.zeros_like(lru_prev)
        lru_carry[...] = jnp.zeros_like(lru_carry)
        state_ref[...] = jnp.zeros_like(state_ref)

    h = h_ref[...]
    p = _mm(_rmsnorm(h, ng_ref[...]), win_ref[...])

    pr = p[:, :RWKV_SHIFT_WIDTH]
    prev = jnp.where(_iota2((rows, 1), 0) == 0, shift_prev[CONV_HALO - 1:CONV_HALO, :], pltpu.roll(pr, 1, 0))
    shift_prev[...] = pr[rows - CONV_HALO:, :]
    ps = pr + mu_ref[...] * (prev - pr)
    W = RWKV_WIDTH
    r, k, v = ps[:, :W], ps[:, W:2 * W], ps[:, 2 * W:3 * W]
    o = 3 * W
    wd, ad, gd = (ps[:, o + i * RWKV_LORA_PAD:o + (i + 1) * RWKV_LORA_PAD] for i in range(3))
    w_log = -jax.nn.softplus(-(w0_ref[...] + _mm(jnp.tanh(wd), wup_ref[...]))) - RWKV_DECAY_OFFSET
    a = jax.nn.sigmoid(a0_ref[...] + _mm(ad, aup_ref[...]))
    gate = _mm(jax.nn.sigmoid(gd), gup_ref[...])
    ones_bd = ones_ref[...]
    kk = k * kk_ref[...]
    kk = kk * lax.rsqrt(_head_sum(kk * kk, ones_bd) + 1e-12)
    k = k * (1.0 + (a - 1.0) * ka_ref[...])
    r_s[...], k_s[...], v_s[...], kk_s[...], a_s[...] = r, k, v, kk, a
    ld_s[...] = -jnp.exp(w_log)

    L = RWKV_CHUNK
    r_i, c_i = _iota2((2 * L, 2 * L), 0), _iota2((2 * L, 2 * L), 1)
    level_masks = []
    b = 1
    while b < L:
        same_block = (r_i & -(2 * b)) == (c_i & -(2 * b))
        level_masks.append(same_block & ((r_i & b) != 0) & ((c_i & b) == 0))
        b *= 2
    t_i, s_i = _iota2((L, L), 0), _iota2((L, L), 1)
    consts = ((t_i >= s_i).astype(F32), r_i > c_i, r_i >= c_i, _iota2((1, 128), 1) < 64, level_masks)
    refs = (r_s, k_s, v_s, kk_s, a_s, ld_s, y_s, state_ref)
    lax.fori_loop(0, rows // L, lambda c, carry: _rwkv_chunk(c, refs, consts), 0)

    y = y_s[...]
    inv_n = 1.0 / RWKV_HEAD_DIM
    mean = _head_sum(y, ones_bd) * inv_n
    yc = y - mean
    var = _head_sum(yc * yc, ones_bd) * inv_n
    y = yc * lax.rsqrt(var + RWKV_GN_EPS) * lng_ref[...] + lnb_ref[...]
    y = y + _head_sum(r * k * rk_ref[...], ones_bd) * v
    y_rwkv = y * gate

    gate_l = p[:, RWKV_SHIFT_WIDTH:RWKV_SHIFT_WIDTH + LRU_WIDTH]
    xb = _causal_conv(p[:, RWKV_SHIFT_WIDTH + LRU_WIDTH:], lru_prev, lcw_ref[...], lcb_ref[...], rows)
    gates = jax.nn.sigmoid(_mm(xb, wax_ref[...]) + bax_ref[...])
    log_a = -LRU_C * gates[:, :LRU_WIDTH] * jax.nn.softplus(-lam_ref[...])
    a_l = jnp.exp(log_a)
    b_l = jnp.sqrt(-jnp.tanh(log_a) * (a_l * a_l + 1.0)) * gates[:, LRU_WIDTH:] * xb
    y_lru = _lru_scan(a_l, b_l, lru_carry, rows) * jax.nn.gelu(gate_l)

    wout = wout_ref[...]
    o_ref[...] = h + _mm(y_rwkv, wout[:RWKV_WIDTH, :]) + _mm(y_lru, wout[RWKV_WIDTH:, :])


def _pad_rows(w, n):
    return jnp.pad(w, ((0, n - w.shape[0]), (0, 0)))


def _pad_cols(w, n):
    return jnp.pad(w, ((0, 0), (0, n - w.shape[-1])))


def _odd_layer(h, ng, w_in, w_out, mu, w0, w_up, a0, a_up, g_up, k_k, k_a, r_k, ln_g, ln_b, lconv_w, lconv_b,
               wa, ba, wx, bx, lam, rows):
    W, P = RWKV_WIDTH, RWKV_LORA_PAD
    o = 3 * W
    dr, ir = w_up.shape[0], a_up.shape[0]

    def pad_sections(m):
        return jnp.concatenate([m[..., :o], _pad_cols(m[..., o:o + dr], P), _pad_cols(m[..., o + dr:o + dr + ir], P),
                                m[..., o + dr + ir:]], axis=-1)

    w_in_p = pad_sections(w_in).astype(BF16)
    mu_p = pad_sections(mu.reshape(1, -1))
    ones_bd = _block_diag(jnp.ones((W // RWKV_HEAD_DIM, RWKV_HEAD_DIM, RWKV_HEAD_DIM), F32))
    wax = jnp.concatenate([_block_diag(wa), _block_diag(wx)], axis=1).astype(BF16)
    consts = [_row(ng), w_in_p, w_out.astype(BF16), mu_p.astype(F32), _row(w0), _pad_rows(w_up, P).astype(BF16),
              _row(a0), _pad_rows(a_up, P).astype(BF16), g_up.astype(BF16), _row(k_k), _row(k_a), _row(r_k),
              _row(ln_g), _row(ln_b), lconv_w.astype(F32), _row(lconv_b), wax,
              _row(jnp.concatenate([ba, bx])), _row(lam), ones_bd]
    scratch = [pltpu.VMEM((CONV_HALO, RWKV_SHIFT_WIDTH), F32), pltpu.VMEM((CONV_HALO, LRU_WIDTH), F32),
               pltpu.VMEM((CONV_HALO, LRU_WIDTH), F32), pltpu.VMEM((RWKV_PAIRS, 128, 128), F32)]
    scratch += [pltpu.VMEM((rows, W), F32) for _ in range(7)]
    return _seq_call(_odd_body, "odd_mixer", h, consts, scratch, rows)


def kernel(x, mix_norm_g, ffn_norm_g, ffn_w_gate, ffn_w_up, ffn_w_down, final_norm_g, ev_w_in, ev_w_out, pool_w,
           pool_scale, ssd_conv_w, ssd_conv_b, ssd_dt_bias, ssd_a_log, ssd_d, ssd_norm_g, od_w_in, od_w_out,
           rwkv_mu, rwkv_w0, rwkv_w_up, rwkv_a0, rwkv_a_up, rwkv_g_up, rwkv_k_k, rwkv_k_a, rwkv_r_k, rwkv_ln_g,
           rwkv_ln_b, lru_conv_w, lru_conv_b, lru_wa, lru_ba, lru_wx, lru_bx, lru_lambda):
    bsz, seq, d = x.shape
    assert d == D_MODEL and x.dtype == F32
    rows = min(SEQ_TILE, seq)
    assert seq % rows == 0 and rows % SSD_CHUNK == 0
    depth = mix_norm_g.shape[0]
    h = x
    for layer in range(depth):
        i = layer // 2
        if layer % 2 == 0:
            h = _even_layer(h, mix_norm_g[layer], ev_w_in[i], ev_w_out[i], pool_w[i], pool_scale[i], ssd_conv_w[i],
                            ssd_conv_b[i], ssd_dt_bias[i], ssd_a_log[i], ssd_d[i], ssd_norm_g[i], rows)
        else:
            h = _odd_layer(h, mix_norm_g[layer], od_w_in[i], od_w_out[i], rwkv_mu[i], rwkv_w0[i], rwkv_w_up[i],
                           rwkv_a0[i], rwkv_a_up[i], rwkv_g_up[i], rwkv_k_k[i], rwkv_k_a[i],
                           rwkv_r_k[i].reshape(-1), rwkv_ln_g[i], rwkv_ln_b[i], lru_conv_w[i], lru_conv_b[i],
                           lru_wa[i], lru_ba[i], lru_wx[i], lru_bx[i], lru_lambda[i], rows)
        final = layer == depth - 1
        h = _ffn(h.reshape(bsz * seq, d), _row(ffn_norm_g[layer]), ffn_w_gate[layer].astype(BF16),
                 ffn_w_up[layer].astype(BF16), ffn_w_down[layer].astype(BF16), _row(final_norm_g),
                 final).reshape(bsz, seq, d)
    return h
```

```python
import functools

import jax
import jax.numpy as jnp
from jax import lax
from jax.experimental import pallas as pl
from jax.experimental.pallas import tpu as pltpu

F32 = jnp.float32
BF16 = jnp.bfloat16

D_MODEL = 1024
D_FF = 2816
RMS_EPS = 1e-6

POOL_WIDTH = 256
POOL_WINDOWS = (2, 4, 8, 16)
POOL_GROUP_DIM = 64
POOL_HALO = 16

SSD_WIDTH = 768
SSD_HEADS = 12
SSD_HEAD_DIM = 64
SSD_GROUPS = 2
SSD_GROUP_WIDTH = SSD_WIDTH // SSD_GROUPS
SSD_STATE = 128
SSD_CHUNK = 128
SSD_CONV_DIM = 1280
CONV_K = 4
CONV_HALO = 8

RWKV_WIDTH = 512
RWKV_HEAD_DIM = 64
RWKV_PAIRS = RWKV_WIDTH // 128
RWKV_CHUNK = 64
RWKV_PREP_CHUNKS = 4
RWKV_LORA_PAD = 128
RWKV_SHIFT_WIDTH = 3 * RWKV_WIDTH + 3 * RWKV_LORA_PAD
RWKV_GN_EPS = 64e-5
RWKV_DECAY_OFFSET = 0.5
LRU_WIDTH = 512
LRU_C = 8.0

EVEN_IN_PAD = POOL_WIDTH + SSD_WIDTH + SSD_CONV_DIM + SSD_WIDTH
ODD_IN_PAD = RWKV_SHIFT_WIDTH + 2 * LRU_WIDTH

SEQ_TILE = 512
FFN_ROWS = 512
VMEM_LIMIT = 56 * 1024 * 1024


def _mm(a, b):
    return jnp.dot(a.astype(BF16), b.astype(BF16), preferred_element_type=F32)


def _mm_nt(a, b):
    return lax.dot_general(a.astype(BF16), b.astype(BF16), (((1,), (1,)), ((), ())),
                           preferred_element_type=F32)


def _mm_tn(a, b):
    return lax.dot_general(a.astype(BF16), b.astype(BF16), (((0,), (0,)), ((), ())),
                           preferred_element_type=F32)


def _split_bf16(x, parts):
    out = []
    for _ in range(parts - 1):
        hi = x.astype(BF16)
        out.append(hi)
        x = x - hi.astype(F32)
    out.append(x.astype(BF16))
    return out


def _cumsum_rows(tri, x):
    return sum(jnp.dot(tri, part, preferred_element_type=F32) for part in _split_bf16(x, 3))


def _head_sum(x, ones_bd):
    return jnp.dot(x.astype(BF16), ones_bd, preferred_element_type=F32)


def _rmsnorm(x, g):
    ms = jnp.mean(x * x, axis=-1, keepdims=True)
    return x * lax.rsqrt(ms + RMS_EPS) * g


def _silu(x):
    return x * jax.nn.sigmoid(x)


def _iota2(shape, axis):
    return lax.broadcasted_iota(jnp.int32, shape, axis)


def _causal_conv(x, prev_ref, w, b, rows):
    xe = jnp.concatenate([prev_ref[...], x], axis=0)
    prev_ref[...] = x[rows - CONV_HALO:, :]
    acc = xe * w[CONV_K - 1:CONV_K, :]
    for j in range(1, CONV_K):
        acc = acc + pltpu.roll(xe, j, 0) * w[CONV_K - 1 - j:CONV_K - j, :]
    return acc[CONV_HALO:, :] + b


def _ffn_body(h_ref, g_ref, wg_ref, wu_ref, wd_ref, fg_ref, o_ref, *, final):
    h = h_ref[...]
    hn = _rmsnorm(h, g_ref[...]).astype(BF16)
    gate = jnp.dot(hn, wg_ref[...], preferred_element_type=F32)
    up = jnp.dot(hn, wu_ref[...], preferred_element_type=F32)
    act = (_silu(gate) * up).astype(BF16)
    out = h + jnp.dot(act, wd_ref[...], preferred_element_type=F32)
    if final:
        out = _rmsnorm(out, fg_ref[...])
    o_ref[...] = out


def _resident(shape):
    nd = len(shape)
    return pl.BlockSpec(shape, lambda *_: (0,) * nd, pipeline_mode=pl.Buffered(1))


def _ffn(h2d, g, wg, wu, wd, fg, final):
    t = h2d.shape[0]
    rows = min(FFN_ROWS, t)
    row_spec = pl.BlockSpec((rows, D_MODEL), lambda i: (i, 0))
    return pl.pallas_call(
        functools.partial(_ffn_body, final=final),
        grid=(t // rows,),
        in_specs=[row_spec, _resident((1, D_MODEL)), _resident((D_MODEL, D_FF)), _resident((D_MODEL, D_FF)),
                  _resident((D_FF, D_MODEL)), _resident((1, D_MODEL))],
        out_specs=row_spec,
        out_shape=jax.ShapeDtypeStruct((t, D_MODEL), F32),
        compiler_params=pltpu.CompilerParams(dimension_semantics=("arbitrary",), vmem_limit_bytes=VMEM_LIMIT),
        name="ffn_final" if final else "ffn",
    )(h2d, g, wg, wu, wd, fg)


def _pool_mixer(u, prev_ref, poolw_bd, pscale, first_pos, rows):
    ue = jnp.concatenate([prev_ref[...], u], axis=0)
    prev_ref[...] = u[rows - POOL_HALO:, :]
    sums = []
    acc = ue
    shift = 1
    for _ in POOL_WINDOWS:
        acc = acc + pltpu.roll(acc, shift, 0)
        sums.append(acc[POOL_HALO:, :])
        shift *= 2
    pos = (first_pos + 1 + _iota2((rows, 1), 0)).astype(F32)
    lane = _iota2((1, POOL_WIDTH), 1)
    pooled = sums[-1] * (1.0 / jnp.minimum(pos, float(POOL_WINDOWS[-1])))
    for gi in range(len(POOL_WINDOWS) - 2, -1, -1):
        cand = sums[gi] * (1.0 / jnp.minimum(pos, float(POOL_WINDOWS[gi])))
        pooled = jnp.where(lane < (gi + 1) * POOL_GROUP_DIM, cand, pooled)
    return _mm(pooled - u, poolw_bd) * pscale


def _ssd_chunk(c, refs, consts):
    xc_s, da_s, b_s, c_s, y_s, state_ref = refs
    tri, causal, lane_lo = consts
    L = SSD_CHUNK
    rows = pl.ds(pl.multiple_of(c * L, L), L)
    cum = _cumsum_rows(tri, da_s[rows, :])
    xc = xc_s[rows, :]
    bmat = b_s[rows, :]
    cmat = c_s[rows, :]
    last = cum[L - 1:L, :]
    ecum = jnp.exp(cum)
    to_end = jnp.exp(last - cum)
    chunk_decay = jnp.exp(last)
    for g in range(SSD_GROUPS):
        gs = slice(g * SSD_GROUP_WIDTH, (g + 1) * SSD_GROUP_WIDTH)
        ns = slice(g * SSD_STATE, (g + 1) * SSD_STATE)
        bg, cg = bmat[:, ns], cmat[:, ns]
        cb = _mm_nt(cg, bg)
        hprev = state_ref[g]
        y_g = _mm(cg, hprev) * ecum[:, gs]
        ypairs = []
        for k in range(SSD_GROUP_WIDTH // 128):
            ps = slice(g * SSD_GROUP_WIDTH + k * 128, g * SSD_GROUP_WIDTH + (k + 1) * 128)
            slab = cum[:, ps]
            swapped = pltpu.roll(slab, 64, 1)
            xpair = xc[:, ps]
            outs = []
            for col in (jnp.where(lane_lo, slab, swapped), jnp.where(lane_lo, swapped, slab)):
                seg = col - col.T
                decay = jnp.where(causal, jnp.exp(jnp.minimum(seg, 0.0)), 0.0)
                outs.append(_mm(cb * decay, xpair))
            ypairs.append(jnp.where(lane_lo, outs[0], outs[1]))
        y_s[rows, gs] = y_g + jnp.concatenate(ypairs, axis=1)
        state_ref[g] = hprev * chunk_decay[:, gs] + _mm_tn(bg, xc[:, gs] * to_end[:, gs])
    return 0


def _even_body(h_ref, ng_ref, win_ref, wout_ref, poolw_ref, pscale_ref, convw_ref, convb_ref, dtb_ref, alog_ref,
               dskip_ref, ssdg_ref, o_ref, upool_prev, xbc_prev, state_ref, xc_s, da_s, b_s, c_s, y_s, *, rows):
    s = pl.program_id(1)

    @pl.when(s == 0)
    def _():
        upool_prev[...] = jnp.zeros_like(upool_prev)
        xbc_prev[...] = jnp.zeros_like(xbc_prev)
        state_ref[...] = jnp.zeros_like(state_ref)

    h = h_ref[...]
    p = _mm(_rmsnorm(h, ng_ref[...]), win_ref[...])
    o1, o2, o3 = POOL_WIDTH, POOL_WIDTH + SSD_WIDTH, POOL_WIDTH + SSD_WIDTH + SSD_CONV_DIM

    y_pool = _pool_mixer(p[:, :o1], upool_prev, poolw_ref[...], pscale_ref[...], s * rows, rows)

    z = p[:, o1:o2]
    xbc = _silu(_causal_conv(p[:, o2:o3], xbc_prev, convw_ref[...], convb_ref[...], rows))
    xh = xbc[:, :SSD_WIDTH]
    dt = jax.nn.softplus(p[:, o3:] + dtb_ref[...])
    xc_s[...] = xh * dt
    da_s[...] = dt * (-jnp.exp(alog_ref[...]))
    b_s[...] = xbc[:, SSD_WIDTH:SSD_WIDTH + SSD_GROUPS * SSD_STATE]
    c_s[...] = xbc[:, SSD_WIDTH + SSD_GROUPS * SSD_STATE:]

    L = SSD_CHUNK
    r_i, c_i = _iota2((L, L), 0), _iota2((L, L), 1)
    consts = ((r_i >= c_i).astype(BF16), r_i >= c_i, _iota2((1, 128), 1) < 64)
    refs = (xc_s, da_s, b_s, c_s, y_s, state_ref)
    lax.fori_loop(0, rows // L, lambda c, carry: _ssd_chunk(c, refs, consts), 0)

    y = (y_s[...] + dskip_ref[...] * xh) * _silu(z)
    normed = []
    for g in range(SSD_GROUPS):
        yy = y[:, g * SSD_GROUP_WIDTH:(g + 1) * SSD_GROUP_WIDTH]
        normed.append(yy * lax.rsqrt(jnp.mean(yy * yy, axis=-1, keepdims=True) + RMS_EPS))
    y_ssd = jnp.concatenate(normed, axis=1) * ssdg_ref[...]

    wout = wout_ref[...]
    o_ref[...] = h + _mm(y_pool, wout[:POOL_WIDTH, :]) + _mm(y_ssd, wout[POOL_WIDTH:, :])


def _seq_call(body, name, h, consts, scratch, rows):
    bsz, seq, _ = h.shape
    tile = pl.BlockSpec((pl.Squeezed(), rows, D_MODEL), lambda b, s: (b, s, 0))
    return pl.pallas_call(
        functools.partial(body, rows=rows),
        grid=(bsz, seq // rows),
        in_specs=[tile] + [_resident(c.shape) for c in consts],
        out_specs=tile,
        out_shape=jax.ShapeDtypeStruct(h.shape, F32),
        scratch_shapes=scratch,
        compiler_params=pltpu.CompilerParams(dimension_semantics=("arbitrary", "arbitrary"),
                                             vmem_limit_bytes=VMEM_LIMIT),
        name=name,
    )(h, *consts)


def _block_diag(w):
    g, c, _ = w.shape
    eye = jnp.eye(g, dtype=w.dtype)
    return (eye[:, None, :, None] * w[:, :, None, :]).reshape(g * c, g * c)


def _row(v):
    return v.reshape(1, -1).astype(F32)


def _even_layer(h, ng, w_in, w_out, pool_w, pool_scale, conv_w, conv_b, dt_bias, a_log, d_skip, norm_g, rows):
    o3 = POOL_WIDTH + SSD_WIDTH + SSD_CONV_DIM
    rep = lambda v: jnp.repeat(v, SSD_HEAD_DIM, axis=-1)
    w_in_p = jnp.concatenate([w_in[:, :o3], rep(w_in[:, o3:])], axis=1).astype(BF16)
    consts = [_row(ng), w_in_p, w_out.astype(BF16), _block_diag(pool_w).astype(BF16), _row(pool_scale),
              conv_w.astype(F32), _row(conv_b), _row(rep(dt_bias)), _row(rep(a_log)), _row(rep(d_skip)),
              _row(norm_g)]
    scratch = [pltpu.VMEM((POOL_HALO, POOL_WIDTH), F32), pltpu.VMEM((CONV_HALO, SSD_CONV_DIM), F32),
               pltpu.VMEM((SSD_GROUPS, SSD_STATE, SSD_GROUP_WIDTH), F32),
               pltpu.VMEM((rows, SSD_WIDTH), F32), pltpu.VMEM((rows, SSD_WIDTH), F32),
               pltpu.VMEM((rows, SSD_GROUPS * SSD_STATE), F32), pltpu.VMEM((rows, SSD_GROUPS * SSD_STATE), F32),
               pltpu.VMEM((rows, SSD_WIDTH), F32)]
    return _seq_call(_even_body, "even_mixer", h, consts, scratch, rows)


def _stack_heads(x, lane_lo):
    return jnp.concatenate([jnp.where(lane_lo, x, 0.0), jnp.where(lane_lo, 0.0, x)], axis=0)


def _chunk_dot(a, b, ca, cb):
    return lax.dot_general(a.astype(BF16), b.astype(BF16), (((ca,), (cb,)), ((), ())), preferred_element_type=F32)


def _rwkv_prepare_chunk(c, refs, consts):
    r_s, k_s, v_s, kk_s, a_s, ld_s, rhat_s, yv_s, mt_s, gt_s, dec_s = refs
    tri, strict, incl, lane_lo, level_masks, eye = consts
    L = RWKV_CHUNK
    ops = []
    for j in range(RWKV_PREP_CHUNKS):
        cj = c * RWKV_PREP_CHUNKS + j
        rows = pl.ds(pl.multiple_of(cj * L, L), L)
        ld = ld_s[rows, :]
        cum = _cumsum_rows(tri, ld)
        last = cum[L - 1:L, :]
        e_neg, e_end = jnp.exp(-cum), jnp.exp(last - cum)
        kk = kk_s[rows, :]
        kka = kk * a_s[rows, :]
        k = k_s[rows, :]
        ops.append(dict(a=-kk * jnp.exp(cum - ld), r=r_s[rows, :] * jnp.exp(cum), b=kka * e_neg, k=k * e_neg,
                        be=kka * e_end, ke=k * e_end, v=v_s[rows, :]))
        dec_s[pl.ds(pl.multiple_of(cj * 8, 8), 8), :] = jnp.broadcast_to(jnp.exp(last), (8, RWKV_WIDTH))

    items = [(j, q) for j in range(RWKV_PREP_CHUNKS) for q in range(RWKV_PAIRS)]
    P = range(len(items))
    st = lambda name, i: _stack_heads(ops[items[i][0]][name][:, items[i][1] * 128:(items[i][1] + 1) * 128], lane_lo)
    a2 = [st("a", i) for i in P]
    r2 = [st("r", i) for i in P]
    bk2 = [jnp.concatenate([st("b", i), st("k", i)], axis=0) for i in P]
    v2 = [st("v", i) for i in P]
    a_bk = [_chunk_dot(a2[i], bk2[i], 1, 1) for i in P]
    r_bk = [_chunk_dot(r2[i], bk2[i], 1, 1) for i in P]
    n_ab = [jnp.where(strict, a_bk[i][:, :2 * L], 0.0) for i in P]
    n_ak = [jnp.where(strict, a_bk[i][:, 2 * L:], 0.0) for i in P]
    n_rb = [jnp.where(incl, r_bk[i][:, :2 * L], 0.0) for i in P]
    n_rk = [jnp.where(incl, r_bk[i][:, 2 * L:], 0.0) for i in P]

    x = [eye + jnp.where(level_masks[0], n_ab[i], 0.0) for i in P]
    for m in level_masks[1:]:
        t = [_chunk_dot(jnp.where(m, n_ab[i], 0.0), x[i], 1, 0) for i in P]
        x = [x[i] + _chunk_dot(x[i], t[i], 1, 0) for i in P]

    akv = [_chunk_dot(n_ak[i], v2[i], 1, 0) for i in P]
    au = [_chunk_dot(x[i], jnp.concatenate([a2[i], akv[i]], axis=1), 1, 0) for i in P]
    ru = [_chunk_dot(n_rb[i], au[i], 1, 0) for i in P]
    rkv = [_chunk_dot(n_rk[i], v2[i], 1, 0) for i in P]
    mg = [_chunk_dot(au[i], st("be", i), 0, 0) for i in P]
    vk = [_chunk_dot(v2[i], st("ke", i), 0, 0) for i in P]
    for i in P:
        idx = (c * RWKV_PREP_CHUNKS + items[i][0]) * RWKV_PAIRS + items[i][1]
        rhat_s[idx] = (r2[i] + ru[i][:, :2 * L]).astype(BF16)
        yv = ru[i][:, 2 * L:] + rkv[i]
        yv_s[idx] = yv[:L, :] + yv[L:, :]
        mt_s[idx] = mg[i][:2 * L, :].astype(BF16)
        gt_s[idx] = mg[i][2 * L:, :] + vk[i]
    return 0


def _rwkv_scan_chunk(c, refs):
    rhat_s, yv_s, mt_s, gt_s, dec_s, y_s, state_ref = refs
    L = RWKV_CHUNK
    rows = pl.ds(pl.multiple_of(c * L, L), L)
    decay = dec_s[pl.ds(pl.multiple_of(c * 8, 8), 8), :][:1, :]
    for q in range(RWKV_PAIRS):
        idx = c * RWKV_PAIRS + q
        sl = slice(q * 128, (q + 1) * 128)
        state = state_ref[q]
        y2 = _chunk_dot(rhat_s[idx], state, 1, 1)
        y_s[rows, sl] = y2[:L, :] + y2[L:, :] + yv_s[idx]
        state_ref[q] = state * decay[:, sl] + _chunk_dot(state, mt_s[idx], 1, 0) + gt_s[idx]
    return 0


def _lru_scan(a, b, carry_ref, rows):
    row = _iota2((rows, 1), 0)
    d = 1
    while d < rows:
        valid = row >= d
        b = jnp.where(valid, a * pltpu.roll(b, d, 0) + b, b)
        a = jnp.where(valid, a * pltpu.roll(a, d, 0), a)
        d *= 2
    hseq = b + a * carry_ref[CONV_HALO - 1:CONV_HALO, :]
    carry_ref[...] = hseq[rows - CONV_HALO:, :]
    return hseq


def _odd_body(h_ref, ng_ref, win_ref, wout_ref, mu_ref, w0_ref, wup_ref, a0_ref, aup_ref, gup_ref, kk_ref, ka_ref,
              rk_ref, lng_ref, lnb_ref, lcw_ref, lcb_ref, wax_ref, bax_ref, lam_ref, ones_ref, o_ref,
              shift_prev, lru_prev, lru_carry, state_ref, r_s, k_s, v_s, kk_s, a_s, ld_s, y_s,
              rhat_s, yv_s, mt_s, gt_s, dec_s, *, rows):
    s = pl.program_id(1)

    @pl.when(s == 0)
    def _():
        shift_prev[...] = jnp.zeros_like(shift_prev)
        lru_prev[...] = jnp.zeros_like(lru_prev)
        lru_carry[...] = jnp.zeros_like(lru_carry)
        state_ref[...] = jnp.zeros_like(state_ref)

    h = h_ref[...]
    p = _mm(_rmsnorm(h, ng_ref[...]), win_ref[...])

    pr = p[:, :RWKV_SHIFT_WIDTH]
    prev = jnp.where(_iota2((rows, 1), 0) == 0, shift_prev[CONV_HALO - 1:CONV_HALO, :], pltpu.roll(pr, 1, 0))
    shift_prev[...] = pr[rows - CONV_HALO:, :]
    ps = pr + mu_ref[...] * (prev - pr)
    W = RWKV_WIDTH
    r, k, v = ps[:, :W], ps[:, W:2 * W], ps[:, 2 * W:3 * W]
    o = 3 * W
    wd, ad, gd = (ps[:, o + i * RWKV_LORA_PAD:o + (i + 1) * RWKV_LORA_PAD] for i in range(3))
    w_log = -jax.nn.softplus(-(w0_ref[...] + _mm(jnp.tanh(wd), wup_ref[...]))) - RWKV_DECAY_OFFSET
    a = jax.nn.sigmoid(a0_ref[...] + _mm(ad, aup_ref[...]))
    gate = _mm(jax.nn.sigmoid(gd), gup_ref[...])
    ones_bd = ones_ref[...]
    kk = k * kk_ref[...]
    kk = kk * lax.rsqrt(_head_sum(kk * kk, ones_bd) + 1e-12)
    k = k * (1.0 + (a - 1.0) * ka_ref[...])
    r_s[...], k_s[...], v_s[...], kk_s[...], a_s[...] = r, k, v, kk, a
    ld_s[...] = -jnp.exp(w_log)

    L = RWKV_CHUNK
    r_i, c_i = _iota2((2 * L, 2 * L), 0), _iota2((2 * L, 2 * L), 1)
    level_masks = []
    b = 1
    while b < L:
        same_block = (r_i & -(2 * b)) == (c_i & -(2 * b))
        level_masks.append(same_block & ((r_i & b) != 0) & ((c_i & b) == 0))
        b *= 2
    t_i, s_i = _iota2((L, L), 0), _iota2((L, L), 1)
    consts = ((t_i >= s_i).astype(BF16), r_i > c_i, r_i >= c_i, _iota2((1, 128), 1) < 64, level_masks,
              (r_i == c_i).astype(F32))
    prep_refs = (r_s, k_s, v_s, kk_s, a_s, ld_s, rhat_s, yv_s, mt_s, gt_s, dec_s)
    lax.fori_loop(0, rows // (L * RWKV_PREP_CHUNKS), lambda c, carry: _rwkv_prepare_chunk(c, prep_refs, consts), 0)
    scan_refs = (rhat_s, yv_s, mt_s, gt_s, dec_s, y_s, state_ref)
    lax.fori_loop(0, rows // L, lambda c, carry: _rwkv_scan_chunk(c, scan_refs), 0)

    y = y_s[...]
    inv_n = 1.0 / RWKV_HEAD_DIM
    mean = _head_sum(y, ones_bd) * inv_n
    yc = y - mean
    var = _head_sum(yc * yc, ones_bd) * inv_n
    y = yc * lax.rsqrt(var + RWKV_GN_EPS) * lng_ref[...] + lnb_ref[...]
    y = y + _head_sum(r * k * rk_ref[...], ones_bd) * v
    y_rwkv = y * gate

    gate_l = p[:, RWKV_SHIFT_WIDTH:RWKV_SHIFT_WIDTH + LRU_WIDTH]
    xb = _causal_conv(p[:, RWKV_SHIFT_WIDTH + LRU_WIDTH:], lru_prev, lcw_ref[...], lcb_ref[...], rows)
    gates = jax.nn.sigmoid(_mm(xb, wax_ref[...]) + bax_ref[...])
    log_a = -LRU_C * gates[:, :LRU_WIDTH] * jax.nn.softplus(-lam_ref[...])
    a_l = jnp.exp(log_a)
    b_l = jnp.sqrt(-jnp.tanh(log_a) * (a_l * a_l + 1.0)) * gates[:, LRU_WIDTH:] * xb
    y_lru = _lru_scan(a_l, b_l, lru_carry, rows) * jax.nn.gelu(gate_l)

    wout = wout_ref[...]
    o_ref[...] = h + _mm(y_rwkv, wout[:RWKV_WIDTH, :]) + _mm(y_lru, wout[RWKV_WIDTH:, :])


def _pad_rows(w, n):
    return jnp.pad(w, ((0, n - w.shape[0]), (0, 0)))


def _pad_cols(w, n):
    return jnp.pad(w, ((0, 0), (0, n - w.shape[-1])))


def _odd_layer(h, ng, w_in, w_out, mu, w0, w_up, a0, a_up, g_up, k_k, k_a, r_k, ln_g, ln_b, lconv_w, lconv_b,
               wa, ba, wx, bx, lam, rows):
    W, P = RWKV_WIDTH, RWKV_LORA_PAD
    o = 3 * W
    dr, ir = w_up.shape[0], a_up.shape[0]

    def pad_sections(m):
        return jnp.concatenate([m[..., :o], _pad_cols(m[..., o:o + dr], P), _pad_cols(m[..., o + dr:o + dr + ir], P),
                                m[..., o + dr + ir:]], axis=-1)

    w_in_p = pad_sections(w_in).astype(BF16)
    mu_p = pad_sections(mu.reshape(1, -1))
    ones_bd = _block_diag(jnp.ones((W // RWKV_HEAD_DIM, RWKV_HEAD_DIM, RWKV_HEAD_DIM), BF16))
    wax = jnp.concatenate([_block_diag(wa), _block_diag(wx)], axis=1).astype(BF16)
    consts = [_row(ng), w_in_p, w_out.astype(BF16), mu_p.astype(F32), _row(w0), _pad_rows(w_up, P).astype(BF16),
              _row(a0), _pad_rows(a_up, P).astype(BF16), g_up.astype(BF16), _row(k_k), _row(k_a), _row(r_k),
              _row(ln_g), _row(ln_b), lconv_w.astype(F32), _row(lconv_b), wax,
              _row(jnp.concatenate([ba, bx])), _row(lam), ones_bd]
    scratch = [pltpu.VMEM((CONV_HALO, RWKV_SHIFT_WIDTH), F32), pltpu.VMEM((CONV_HALO, LRU_WIDTH), F32),
               pltpu.VMEM((CONV_HALO, LRU_WIDTH), F32), pltpu.VMEM((RWKV_PAIRS, 128, 128), F32)]
    scratch += [pltpu.VMEM((rows, W), F32) for _ in range(7)]
    n_pc = (rows // RWKV_CHUNK) * RWKV_PAIRS
    scratch += [pltpu.VMEM((n_pc, 128, 128), BF16), pltpu.VMEM((n_pc, RWKV_CHUNK, 128), F32),
                pltpu.VMEM((n_pc, 128, 128), BF16), pltpu.VMEM((n_pc, 128, 128), F32),
                pltpu.VMEM((rows // RWKV_CHUNK * 8, W), F32)]
    return _seq_call(_odd_body, "odd_mixer", h, consts, scratch, rows)


def kernel(x, mix_norm_g, ffn_norm_g, ffn_w_gate, ffn_w_up, ffn_w_down, final_norm_g, ev_w_in, ev_w_out, pool_w,
           pool_scale, ssd_conv_w, ssd_conv_b, ssd_dt_bias, ssd_a_log, ssd_d, ssd_norm_g, od_w_in, od_w_out,
           rwkv_mu, rwkv_w0, rwkv_w_up, rwkv_a0, rwkv_a_up, rwkv_g_up, rwkv_k_k, rwkv_k_a, rwkv_r_k, rwkv_ln_g,
           rwkv_ln_b, lru_conv_w, lru_conv_b, lru_wa, lru_ba, lru_wx, lru_bx, lru_lambda):
    bsz, seq, d = x.shape
    assert d == D_MODEL and x.dtype == F32
    rows = min(SEQ_TILE, seq)
    assert seq % rows == 0 and rows % SSD_CHUNK == 0
    depth = mix_norm_g.shape[0]
    h = x
    for layer in range(depth):
        i = layer // 2
        if layer % 2 == 0:
            h = _even_layer(h, mix_norm_g[layer], ev_w_in[i], ev_w_out[i], pool_w[i], pool_scale[i], ssd_conv_w[i],
                            ssd_conv_b[i], ssd_dt_bias[i], ssd_a_log[i], ssd_d[i], ssd_norm_g[i], rows)
        else:
            h = _odd_layer(h, mix_norm_g[layer], od_w_in[i], od_w_out[i], rwkv_mu[i], rwkv_w0[i], rwkv_w_up[i],
                           rwkv_a0[i], rwkv_a_up[i], rwkv_g_up[i], rwkv_k_k[i], rwkv_k_a[i],
                           rwkv_r_k[i].reshape(-1), rwkv_ln_g[i], rwkv_ln_b[i], lru_conv_w[i], lru_conv_b[i],
                           lru_wa[i], lru_ba[i], lru_wx[i], lru_bx[i], lru_lambda[i], rows)
        final = layer == depth - 1
        h = _ffn(h.reshape(bsz * seq, d), _row(ffn_norm_g[layer]), ffn_w_gate[layer].astype(BF16),
                 ffn_w_up[layer].astype(BF16), ffn_w_down[layer].astype(BF16), _row(final_norm_g),
                 final).reshape(bsz, seq, d)
    return h
```

```python
import functools

import jax
import jax.numpy as jnp
from jax import lax
from jax.experimental import pallas as pl
from jax.experimental.pallas import tpu as pltpu

F32 = jnp.float32
BF16 = jnp.bfloat16

D_MODEL = 1024
D_FF = 2816
RMS_EPS = 1e-6

POOL_WIDTH = 256
POOL_WINDOWS = (2, 4, 8, 16)
POOL_GROUP_DIM = 64
POOL_HALO = 16

SSD_WIDTH = 768
SSD_HEADS = 12
SSD_HEAD_DIM = 64
SSD_GROUPS = 2
SSD_GROUP_WIDTH = SSD_WIDTH // SSD_GROUPS
SSD_STATE = 128
SSD_CHUNK = 128
SSD_CONV_DIM = 1280
CONV_K = 4
CONV_HALO = 8

RWKV_WIDTH = 512
RWKV_HEAD_DIM = 64
RWKV_PAIRS = RWKV_WIDTH // 128
RWKV_CHUNK = 64
RWKV_PREP_CHUNKS = 4
RWKV_LORA_PAD = 128
RWKV_SHIFT_WIDTH = 3 * RWKV_WIDTH + 3 * RWKV_LORA_PAD
RWKV_GN_EPS = 64e-5
RWKV_DECAY_OFFSET = 0.5
LRU_WIDTH = 512
LRU_C = 8.0

EVEN_IN_PAD = POOL_WIDTH + SSD_WIDTH + SSD_CONV_DIM + SSD_WIDTH
ODD_IN_PAD = RWKV_SHIFT_WIDTH + 2 * LRU_WIDTH

SEQ_TILE = 512
FFN_ROWS = 512
VMEM_LIMIT = 56 * 1024 * 1024


def _mm(a, b):
    return jnp.dot(a.astype(BF16), b.astype(BF16), preferred_element_type=F32)


def _mm_nt(a, b):
    return lax.dot_general(a.astype(BF16), b.astype(BF16), (((1,), (1,)), ((), ())),
                           preferred_element_type=F32)


def _mm_tn(a, b):
    return lax.dot_general(a.astype(BF16), b.astype(BF16), (((0,), (0,)), ((), ())),
                           preferred_element_type=F32)


def _split_bf16(x, parts):
    out = []
    for _ in range(parts - 1):
        hi = x.astype(BF16)
        out.append(hi)
        x = x - hi.astype(F32)
    out.append(x.astype(BF16))
    return out


def _cumsum_rows(tri, x, parts):
    return sum(jnp.dot(tri, part, preferred_element_type=F32) for part in _split_bf16(x, parts))


def _head_sum(x, ones_bd):
    return jnp.dot(x.astype(BF16), ones_bd, preferred_element_type=F32)


def _rmsnorm(x, g):
    ms = jnp.mean(x * x, axis=-1, keepdims=True)
    return x * lax.rsqrt(ms + RMS_EPS) * g


def _silu(x):
    return x * jax.nn.sigmoid(x)


def _iota2(shape, axis):
    return lax.broadcasted_iota(jnp.int32, shape, axis)


def _causal_conv(x, prev_ref, w, b, rows):
    xe = jnp.concatenate([prev_ref[...], x], axis=0)
    prev_ref[...] = x[rows - CONV_HALO:, :]
    x1 = pltpu.roll(xe, 1, 0)
    acc = xe * w[3:4, :] + x1 * w[2:3, :] + pltpu.roll(xe * w[1:2, :] + x1 * w[0:1, :], 2, 0)
    return acc[CONV_HALO:, :] + b


def _ffn_body(h_ref, g_ref, wg_ref, wu_ref, wd_ref, fg_ref, o_ref, *, final):
    h = h_ref[...]
    hn = _rmsnorm(h, g_ref[...]).astype(BF16)
    gate = jnp.dot(hn, wg_ref[...], preferred_element_type=F32)
    up = jnp.dot(hn, wu_ref[...], preferred_element_type=F32)
    act = (_silu(gate) * up).astype(BF16)
    out = h + jnp.dot(act, wd_ref[...], preferred_element_type=F32)
    if final:
        out = _rmsnorm(out, fg_ref[...])
    o_ref[...] = out


def _resident(shape):
    nd = len(shape)
    return pl.BlockSpec(shape, lambda *_: (0,) * nd, pipeline_mode=pl.Buffered(1))


def _ffn(h2d, g, wg, wu, wd, fg, final):
    t = h2d.shape[0]
    rows = min(FFN_ROWS, t)
    row_spec = pl.BlockSpec((rows, D_MODEL), lambda i: (i, 0))
    return pl.pallas_call(
        functools.partial(_ffn_body, final=final),
        grid=(t // rows,),
        in_specs=[row_spec, _resident((1, D_MODEL)), _resident((D_MODEL, D_FF)), _resident((D_MODEL, D_FF)),
                  _resident((D_FF, D_MODEL)), _resident((1, D_MODEL))],
        out_specs=row_spec,
        out_shape=jax.ShapeDtypeStruct((t, D_MODEL), F32),
        compiler_params=pltpu.CompilerParams(dimension_semantics=("arbitrary",), vmem_limit_bytes=VMEM_LIMIT),
        name="ffn_final" if final else "ffn",
    )(h2d, g, wg, wu, wd, fg)


def _pool_mixer(u, prev_ref, poolw_bd, pscale, first_pos, rows):
    ue = jnp.concatenate([prev_ref[...], u], axis=0)
    prev_ref[...] = u[rows - POOL_HALO:, :]
    sums = []
    acc = ue
    shift = 1
    for _ in POOL_WINDOWS:
        acc = acc + pltpu.roll(acc, shift, 0)
        sums.append(acc[POOL_HALO:, :])
        shift *= 2
    pos = (first_pos + 1 + _iota2((rows, 1), 0)).astype(F32)
    lane = _iota2((1, POOL_WIDTH), 1)
    pooled = sums[-1] * (1.0 / jnp.minimum(pos, float(POOL_WINDOWS[-1])))
    for gi in range(len(POOL_WINDOWS) - 2, -1, -1):
        cand = sums[gi] * (1.0 / jnp.minimum(pos, float(POOL_WINDOWS[gi])))
        pooled = jnp.where(lane < (gi + 1) * POOL_GROUP_DIM, cand, pooled)
    return _mm(pooled - u, poolw_bd) * pscale


def _ssd_chunk(c, refs, consts):
    xc_s, da_s, b_s, c_s, y_s, state_ref = refs
    tri, causal, lane_lo = consts
    L = SSD_CHUNK
    rows = pl.ds(pl.multiple_of(c * L, L), L)
    cum = _cumsum_rows(tri, da_s[rows, :], 2)
    xc = xc_s[rows, :]
    bmat = b_s[rows, :]
    cmat = c_s[rows, :]
    last = cum[L - 1:L, :]
    ecum = jnp.exp(cum)
    to_end = jnp.exp(last - cum)
    chunk_decay = jnp.exp(last)
    for g in range(SSD_GROUPS):
        gs = slice(g * SSD_GROUP_WIDTH, (g + 1) * SSD_GROUP_WIDTH)
        ns = slice(g * SSD_STATE, (g + 1) * SSD_STATE)
        bg, cg = bmat[:, ns], cmat[:, ns]
        cb = _mm_nt(cg, bg)
        hprev = state_ref[g]
        y_g = _mm(cg, hprev) * ecum[:, gs]
        ypairs = []
        for k in range(SSD_GROUP_WIDTH // 128):
            ps = slice(g * SSD_GROUP_WIDTH + k * 128, g * SSD_GROUP_WIDTH + (k + 1) * 128)
            slab = cum[:, ps]
            swapped = pltpu.roll(slab, 64, 1)
            xpair = xc[:, ps]
            outs = []
            for col in (jnp.where(lane_lo, slab, swapped), jnp.where(lane_lo, swapped, slab)):
                seg = col - col.T
                decay = jnp.where(causal, jnp.exp(jnp.minimum(seg, 0.0)), 0.0)
                outs.append(_mm(cb * decay, xpair))
            ypairs.append(jnp.where(lane_lo, outs[0], outs[1]))
        y_s[rows, gs] = y_g + jnp.concatenate(ypairs, axis=1)
        state_ref[g] = hprev * chunk_decay[:, gs] + _mm_tn(bg, xc[:, gs] * to_end[:, gs])
    return 0


def _even_body(h_ref, ng_ref, win_ref, wout_ref, poolw_ref, pscale_ref, convw_ref, convb_ref, dtb_ref, alog_ref,
               dskip_ref, ssdg_ref, o_ref, upool_prev, xbc_prev, state_ref, xc_s, da_s, b_s, c_s, y_s, *, rows):
    s = pl.program_id(1)

    @pl.when(s == 0)
    def _():
        upool_prev[...] = jnp.zeros_like(upool_prev)
        xbc_prev[...] = jnp.zeros_like(xbc_prev)
        state_ref[...] = jnp.zeros_like(state_ref)

    h = h_ref[...]
    p = _mm(_rmsnorm(h, ng_ref[...]), win_ref[...])
    o1, o2, o3 = POOL_WIDTH, POOL_WIDTH + SSD_WIDTH, POOL_WIDTH + SSD_WIDTH + SSD_CONV_DIM

    y_pool = _pool_mixer(p[:, :o1], upool_prev, poolw_ref[...], pscale_ref[...], s * rows, rows)

    z = p[:, o1:o2]
    xbc = _silu(_causal_conv(p[:, o2:o3], xbc_prev, convw_ref[...], convb_ref[...], rows))
    xh = xbc[:, :SSD_WIDTH]
    dt = jax.nn.softplus(p[:, o3:] + dtb_ref[...])
    xc_s[...] = xh * dt
    da_s[...] = dt * (-jnp.exp(alog_ref[...]))
    b_s[...] = xbc[:, SSD_WIDTH:SSD_WIDTH + SSD_GROUPS * SSD_STATE]
    c_s[...] = xbc[:, SSD_WIDTH + SSD_GROUPS * SSD_STATE:]

    L = SSD_CHUNK
    r_i, c_i = _iota2((L, L), 0), _iota2((L, L), 1)
    consts = ((r_i >= c_i).astype(BF16), r_i >= c_i, _iota2((1, 128), 1) < 64)
    refs = (xc_s, da_s, b_s, c_s, y_s, state_ref)
    lax.fori_loop(0, rows // L, lambda c, carry: _ssd_chunk(c, refs, consts), 0, unroll=4)

    y = (y_s[...] + dskip_ref[...] * xh) * _silu(z)
    normed = []
    for g in range(SSD_GROUPS):
        yy = y[:, g * SSD_GROUP_WIDTH:(g + 1) * SSD_GROUP_WIDTH]
        normed.append(yy * lax.rsqrt(jnp.mean(yy * yy, axis=-1, keepdims=True) + RMS_EPS))
    y_ssd = jnp.concatenate(normed, axis=1) * ssdg_ref[...]

    wout = wout_ref[...]
    o_ref[...] = h + _mm(y_pool, wout[:POOL_WIDTH, :]) + _mm(y_ssd, wout[POOL_WIDTH:, :])


def _seq_call(body, name, h, consts, scratch, rows):
    bsz, seq, _ = h.shape
    tile = pl.BlockSpec((pl.Squeezed(), rows, D_MODEL), lambda b, s: (b, s, 0))
    return pl.pallas_call(
        functools.partial(body, rows=rows),
        grid=(bsz, seq // rows),
        in_specs=[tile] + [_resident(c.shape) for c in consts],
        out_specs=tile,
        out_shape=jax.ShapeDtypeStruct(h.shape, F32),
        scratch_shapes=scratch,
        compiler_params=pltpu.CompilerParams(dimension_semantics=("arbitrary", "arbitrary"),
                                             vmem_limit_bytes=VMEM_LIMIT),
        name=name,
    )(h, *consts)


def _block_diag(w):
    g, c, _ = w.shape
    eye = jnp.eye(g, dtype=w.dtype)
    return (eye[:, None, :, None] * w[:, :, None, :]).reshape(g * c, g * c)


def _row(v):
    return v.reshape(1, -1).astype(F32)


def _even_layer(h, ng, w_in, w_out, pool_w, pool_scale, conv_w, conv_b, dt_bias, a_log, d_skip, norm_g, rows):
    o3 = POOL_WIDTH + SSD_WIDTH + SSD_CONV_DIM
    rep = lambda v: jnp.repeat(v, SSD_HEAD_DIM, axis=-1)
    w_in_p = jnp.concatenate([w_in[:, :o3], rep(w_in[:, o3:])], axis=1).astype(BF16)
    consts = [_row(ng), w_in_p, w_out.astype(BF16), _block_diag(pool_w).astype(BF16), _row(pool_scale),
              conv_w.astype(F32), _row(conv_b), _row(rep(dt_bias)), _row(rep(a_log)), _row(rep(d_skip)),
              _row(norm_g)]
    scratch = [pltpu.VMEM((POOL_HALO, POOL_WIDTH), F32), pltpu.VMEM((CONV_HALO, SSD_CONV_DIM), F32),
               pltpu.VMEM((SSD_GROUPS, SSD_STATE, SSD_GROUP_WIDTH), F32),
               pltpu.VMEM((rows, SSD_WIDTH), F32), pltpu.VMEM((rows, SSD_WIDTH), F32),
               pltpu.VMEM((rows, SSD_GROUPS * SSD_STATE), F32), pltpu.VMEM((rows, SSD_GROUPS * SSD_STATE), F32),
               pltpu.VMEM((rows, SSD_WIDTH), F32)]
    return _seq_call(_even_body, "even_mixer", h, consts, scratch, rows)


def _stack_heads(x, lane_lo):
    return jnp.concatenate([jnp.where(lane_lo, x, 0.0), jnp.where(lane_lo, 0.0, x)], axis=0)


def _chunk_dot(a, b, ca, cb):
    return lax.dot_general(a.astype(BF16), b.astype(BF16), (((ca,), (cb,)), ((), ())), preferred_element_type=F32)


def _rwkv_prepare_chunk(c, refs, consts):
    r_s, k_s, v_s, kk_s, a_s, ld_s, rhat_s, yv_s, mt_s, gt_s, dec_s = refs
    tri, strict, incl, lane_lo, level_masks, eye = consts
    L = RWKV_CHUNK
    ops = []
    for j in range(RWKV_PREP_CHUNKS):
        cj = c * RWKV_PREP_CHUNKS + j
        rows = pl.ds(pl.multiple_of(cj * L, L), L)
        ld = ld_s[rows, :]
        cum = _cumsum_rows(tri, ld, 3)
        last = cum[L - 1:L, :]
        e_neg, e_end = jnp.exp(-cum), jnp.exp(last - cum)
        kk = kk_s[rows, :]
        kka = kk * a_s[rows, :]
        k = k_s[rows, :]
        ops.append(dict(a=-kk * jnp.exp(cum - ld), r=r_s[rows, :] * jnp.exp(cum), b=kka * e_neg, k=k * e_neg,
                        be=kka * e_end, ke=k * e_end, v=v_s[rows, :]))
        dec_s[pl.ds(pl.multiple_of(cj * 8, 8), 8), :] = jnp.broadcast_to(jnp.exp(last), (8, RWKV_WIDTH))

    items = [(j, q) for j in range(RWKV_PREP_CHUNKS) for q in range(RWKV_PAIRS)]
    P = range(len(items))
    st = lambda name, i: _stack_heads(ops[items[i][0]][name][:, items[i][1] * 128:(items[i][1] + 1) * 128], lane_lo)
    a2 = [st("a", i) for i in P]
    r2 = [st("r", i) for i in P]
    bk2 = [jnp.concatenate([st("b", i), st("k", i)], axis=0) for i in P]
    v2 = [st("v", i) for i in P]
    n_a = [jnp.where(strict, _chunk_dot(a2[i], bk2[i], 1, 1), 0.0) for i in P]
    n_r = [jnp.where(incl, _chunk_dot(r2[i], bk2[i], 1, 1), 0.0) for i in P]
    n_ab = [n_a[i][:, :2 * L] for i in P]

    x = [eye + jnp.where(level_masks[0], n_ab[i], 0.0) for i in P]
    for m in level_masks[1:]:
        t = [_chunk_dot(jnp.where(m, n_ab[i], 0.0), x[i], 1, 0) for i in P]
        x = [x[i] + _chunk_dot(x[i], t[i], 1, 0) for i in P]

    akv = [_chunk_dot(n_a[i][:, 2 * L:], v2[i], 1, 0) for i in P]
    au = [_chunk_dot(x[i], jnp.concatenate([a2[i], akv[i]], axis=1), 1, 0) for i in P]
    ru = [_chunk_dot(n_r[i][:, :2 * L], au[i], 1, 0) for i in P]
    rkv = [_chunk_dot(n_r[i][:, 2 * L:], v2[i], 1, 0) for i in P]
    mg = [_chunk_dot(au[i], st("be", i), 0, 0) for i in P]
    vk = [_chunk_dot(v2[i], st("ke", i), 0, 0) for i in P]
    for i in P:
        idx = (c * RWKV_PREP_CHUNKS + items[i][0]) * RWKV_PAIRS + items[i][1]
        rhat_s[idx] = (r2[i] + ru[i][:, :2 * L]).astype(BF16)
        yv = ru[i][:, 2 * L:] + rkv[i]
        yv_s[idx] = yv[:L, :] + yv[L:, :]
        mt_s[idx] = mg[i][:2 * L, :].astype(BF16)
        gt_s[idx] = mg[i][2 * L:, :] + vk[i]
    return 0


def _rwkv_scan_chunk(c, refs):
    rhat_s, yv_s, mt_s, gt_s, dec_s, y_s, state_ref = refs
    L = RWKV_CHUNK
    rows = pl.ds(pl.multiple_of(c * L, L), L)
    decay = dec_s[pl.ds(pl.multiple_of(c * 8, 8), 8), :][:1, :]
    for q in range(RWKV_PAIRS):
        idx = c * RWKV_PAIRS + q
        sl = slice(q * 128, (q + 1) * 128)
        state = state_ref[q]
        y2 = _chunk_dot(rhat_s[idx], state, 1, 1)
        y_s[rows, sl] = y2[:L, :] + y2[L:, :] + yv_s[idx]
        state_ref[q] = state * decay[:, sl] + _chunk_dot(state, mt_s[idx], 1, 0) + gt_s[idx]
    return 0


def _lru_scan(a, b, carry_ref, rows):
    groups, width = rows // 8, a.shape[1]
    a3, b3 = a.reshape(groups, 8, width), b.reshape(groups, 8, width)
    sub = _iota2((1, 8, 1), 1)
    d = 1
    while d < 8:
        valid = sub >= d
        b3 = jnp.where(valid, a3 * pltpu.roll(b3, d, 1) + b3, b3)
        a3 = jnp.where(valid, a3 * pltpu.roll(a3, d, 1), a3)
        d *= 2
    h_in = carry_ref[CONV_HALO - 1:CONV_HALO, :]
    blocks = []
    for g in range(groups):
        blocks.append(b3[g] + a3[g] * h_in)
        h_in = blocks[-1][7:8, :]
    carry_ref[...] = blocks[-1]
    return jnp.concatenate(blocks, axis=0)


def _odd_body(h_ref, ng_ref, win_ref, wout_ref, mu_ref, w0_ref, wup_ref, a0_ref, aup_ref, gup_ref, kk_ref, ka_ref,
              rk_ref, lng_ref, lnb_ref, lcw_ref, lcb_ref, wax_ref, bax_ref, lam_ref, ones_ref, o_ref,
              shift_prev, lru_prev, lru_carry, state_ref, r_s, k_s, v_s, kk_s, a_s, ld_s, y_s,
              rhat_s, yv_s, mt_s, gt_s, dec_s, *, rows):
    s = pl.program_id(1)

    @pl.when(s == 0)
    def _():
        shift_prev[...] = jnp.zeros_like(shift_prev)
        lru_prev[...] = jnp.zeros_like(lru_prev)
        lru_carry[...] = jnp.zeros_like(lru_carry)
        state_ref[...] = jnp.zeros_like(state_ref)

    h = h_ref[...]
    p = _mm(_rmsnorm(h, ng_ref[...]), win_ref[...])

    pr = p[:, :RWKV_SHIFT_WIDTH]
    prev = jnp.where(_iota2((rows, 1), 0) == 0, shift_prev[CONV_HALO - 1:CONV_HALO, :], pltpu.roll(pr, 1, 0))
    shift_prev[...] = pr[rows - CONV_HALO:, :]
    ps = pr + mu_ref[...] * (prev - pr)
    W = RWKV_WIDTH
    r, k, v = ps[:, :W], ps[:, W:2 * W], ps[:, 2 * W:3 * W]
    o = 3 * W
    wd, ad, gd = (ps[:, o + i * RWKV_LORA_PAD:o + (i + 1) * RWKV_LORA_PAD] for i in range(3))
    w_log = -jax.nn.softplus(-(w0_ref[...] + _mm(jnp.tanh(wd), wup_ref[...]))) - RWKV_DECAY_OFFSET
    a = jax.nn.sigmoid(a0_ref[...] + _mm(ad, aup_ref[...]))
    gate = _mm(jax.nn.sigmoid(gd), gup_ref[...])
    ones_bd = ones_ref[...]
    kk = k * kk_ref[...]
    kk = kk * lax.rsqrt(_head_sum(kk * kk, ones_bd) + 1e-12)
    k = k * (1.0 + (a - 1.0) * ka_ref[...])
    r_s[...], k_s[...], v_s[...], kk_s[...], a_s[...] = r, k, v, kk, a
    ld_s[...] = -jnp.exp(w_log)

    L = RWKV_CHUNK
    r_i, c_i = _iota2((2 * L, 2 * L), 0), _iota2((2 * L, 2 * L), 1)
    level_masks = []
    b = 1
    while b < L:
        same_block = (r_i & -(2 * b)) == (c_i & -(2 * b))
        level_masks.append(same_block & ((r_i & b) != 0) & ((c_i & b) == 0))
        b *= 2
    t_i, s_i = _iota2((L, L), 0), _iota2((L, L), 1)
    r_w, c_w = _iota2((2 * L, 4 * L), 0), _iota2((2 * L, 4 * L), 1) & (2 * L - 1)
    consts = ((t_i >= s_i).astype(BF16), r_w > c_w, r_w >= c_w, _iota2((1, 128), 1) < 64, level_masks,
              (r_i == c_i).astype(F32))
    prep_refs = (r_s, k_s, v_s, kk_s, a_s, ld_s, rhat_s, yv_s, mt_s, gt_s, dec_s)
    lax.fori_loop(0, rows // (L * RWKV_PREP_CHUNKS), lambda c, carry: _rwkv_prepare_chunk(c, prep_refs, consts), 0)
    scan_refs = (rhat_s, yv_s, mt_s, gt_s, dec_s, y_s, state_ref)
    lax.fori_loop(0, rows // L, lambda c, carry: _rwkv_scan_chunk(c, scan_refs), 0, unroll=4)

    y = y_s[...]
    inv_n = 1.0 / RWKV_HEAD_DIM
    mean = _head_sum(y, ones_bd) * inv_n
    yc = y - mean
    var = _head_sum(yc * yc, ones_bd) * inv_n
    y = yc * lax.rsqrt(var + RWKV_GN_EPS) * lng_ref[...] + lnb_ref[...]
    y = y + _head_sum(r * k * rk_ref[...], ones_bd) * v
    y_rwkv = y * gate

    gate_l = p[:, RWKV_SHIFT_WIDTH:RWKV_SHIFT_WIDTH + LRU_WIDTH]
    xb = _causal_conv(p[:, RWKV_SHIFT_WIDTH + LRU_WIDTH:], lru_prev, lcw_ref[...], lcb_ref[...], rows)
    gates = jax.nn.sigmoid(_mm(xb, wax_ref[...]) + bax_ref[...])
    log_a = -LRU_C * gates[:, :LRU_WIDTH] * jax.nn.softplus(-lam_ref[...])
    a_l = jnp.exp(log_a)
    b_l = jnp.sqrt(-jnp.tanh(log_a) * (a_l * a_l + 1.0)) * gates[:, LRU_WIDTH:] * xb
    y_lru = _lru_scan(a_l, b_l, lru_carry, rows) * jax.nn.gelu(gate_l)

    wout = wout_ref[...]
    o_ref[...] = h + _mm(y_rwkv, wout[:RWKV_WIDTH, :]) + _mm(y_lru, wout[RWKV_WIDTH:, :])


def _pad_rows(w, n):
    return jnp.pad(w, ((0, n - w.shape[0]), (0, 0)))


def _pad_cols(w, n):
    return jnp.pad(w, ((0, 0), (0, n - w.shape[-1])))


def _odd_layer(h, ng, w_in, w_out, mu, w0, w_up, a0, a_up, g_up, k_k, k_a, r_k, ln_g, ln_b, lconv_w, lconv_b,
               wa, ba, wx, bx, lam, rows):
    W, P = RWKV_WIDTH, RWKV_LORA_PAD
    o = 3 * W
    dr, ir = w_up.shape[0], a_up.shape[0]

    def pad_sections(m):
        return jnp.concatenate([m[..., :o], _pad_cols(m[..., o:o + dr], P), _pad_cols(m[..., o + dr:o + dr + ir], P),
                                m[..., o + dr + ir:]], axis=-1)

    w_in_p = pad_sections(w_in).astype(BF16)
    mu_p = pad_sections(mu.reshape(1, -1))
    ones_bd = _block_diag(jnp.ones((W // RWKV_HEAD_DIM, RWKV_HEAD_DIM, RWKV_HEAD_DIM), BF16))
    wax = jnp.concatenate([_block_diag(wa), _block_diag(wx)], axis=1).astype(BF16)
    consts = [_row(ng), w_in_p, w_out.astype(BF16), mu_p.astype(F32), _row(w0), _pad_rows(w_up, P).astype(BF16),
              _row(a0), _pad_rows(a_up, P).astype(BF16), g_up.astype(BF16), _row(k_k), _row(k_a), _row(r_k),
              _row(ln_g), _row(ln_b), lconv_w.astype(F32), _row(lconv_b), wax,
              _row(jnp.concatenate([ba, bx])), _row(lam), ones_bd]
    scratch = [pltpu.VMEM((CONV_HALO, RWKV_SHIFT_WIDTH), F32), pltpu.VMEM((CONV_HALO, LRU_WIDTH), F32),
               pltpu.VMEM((CONV_HALO, LRU_WIDTH), F32), pltpu.VMEM((RWKV_PAIRS, 128, 128), F32)]
    scratch += [pltpu.VMEM((rows, W), F32) for _ in range(7)]
    n_pc = (rows // RWKV_CHUNK) * RWKV_PAIRS
    scratch += [pltpu.VMEM((n_pc, 128, 128), BF16), pltpu.VMEM((n_pc, RWKV_CHUNK, 128), F32),
                pltpu.VMEM((n_pc, 128, 128), BF16), pltpu.VMEM((n_pc, 128, 128), F32),
                pltpu.VMEM((rows // RWKV_CHUNK * 8, W), F32)]
    return _seq_call(_odd_body, "odd_mixer", h, consts, scratch, rows)


def kernel(x, mix_norm_g, ffn_norm_g, ffn_w_gate, ffn_w_up, ffn_w_down, final_norm_g, ev_w_in, ev_w_out, pool_w,
           pool_scale, ssd_conv_w, ssd_conv_b, ssd_dt_bias, ssd_a_log, ssd_d, ssd_norm_g, od_w_in, od_w_out,
           rwkv_mu, rwkv_w0, rwkv_w_up, rwkv_a0, rwkv_a_up, rwkv_g_up, rwkv_k_k, rwkv_k_a, rwkv_r_k, rwkv_ln_g,
           rwkv_ln_b, lru_conv_w, lru_conv_b, lru_wa, lru_ba, lru_wx, lru_bx, lru_lambda):
    bsz, seq, d = x.shape
    assert d == D_MODEL and x.dtype == F32
    rows = min(SEQ_TILE, seq)
    assert seq % rows == 0 and rows % SSD_CHUNK == 0
    depth = mix_norm_g.shape[0]
    h = x
    for layer in range(depth):
        i = layer // 2
        if layer % 2 == 0:
            h = _even_layer(h, mix_norm_g[layer], ev_w_in[i], ev_w_out[i], pool_w[i], pool_scale[i], ssd_conv_w[i],
                            ssd_conv_b[i], ssd_dt_bias[i], ssd_a_log[i], ssd_d[i], ssd_norm_g[i], rows)
        else:
            h = _odd_layer(h, mix_norm_g[layer], od_w_in[i], od_w_out[i], rwkv_mu[i], rwkv_w0[i], rwkv_w_up[i],
                           rwkv_a0[i], rwkv_a_up[i], rwkv_g_up[i], rwkv_k_k[i], rwkv_k_a[i],
                           rwkv_r_k[i].reshape(-1), rwkv_ln_g[i], rwkv_ln_b[i], lru_conv_w[i], lru_conv_b[i],
                           lru_wa[i], lru_ba[i], lru_wx[i], lru_bx[i], lru_lambda[i], rows)
        final = layer == depth - 1
        h = _ffn(h.reshape(bsz * seq, d), _row(ffn_norm_g[layer]), ffn_w_gate[layer].astype(BF16),
                 ffn_w_up[layer].astype(BF16), ffn_w_down[layer].astype(BF16), _row(final_norm_g),
                 final).reshape(bsz, seq, d)
    return h
```

```python
import functools

import jax
import jax.numpy as jnp
from jax import lax
from jax.experimental import pallas as pl
from jax.experimental.pallas import tpu as pltpu

F32 = jnp.float32
BF16 = jnp.bfloat16

D_MODEL = 1024
D_FF = 2816
RMS_EPS = 1e-6

POOL_WIDTH = 256
POOL_WINDOWS = (2, 4, 8, 16)
POOL_GROUP_DIM = 64
POOL_HALO = 16

SSD_WIDTH = 768
SSD_HEADS = 12
SSD_HEAD_DIM = 64
SSD_GROUPS = 2
SSD_GROUP_WIDTH = SSD_WIDTH // SSD_GROUPS
SSD_STATE = 128
SSD_CHUNK = 128
SSD_CONV_DIM = 1280
CONV_K = 4
CONV_HALO = 8

RWKV_WIDTH = 512
RWKV_HEAD_DIM = 64
RWKV_PAIRS = RWKV_WIDTH // 128
RWKV_CHUNK = 64
RWKV_PREP_CHUNKS = 4
RWKV_LORA_PAD = 128
RWKV_SHIFT_WIDTH = 3 * RWKV_WIDTH + 3 * RWKV_LORA_PAD
RWKV_GN_EPS = 64e-5
RWKV_DECAY_OFFSET = 0.5
LRU_WIDTH = 512
LRU_C = 8.0

SSD_DT_PAD = 128
ODD_IN_PAD = RWKV_SHIFT_WIDTH + 2 * LRU_WIDTH

SEQ_TILE = 512
FFN_ROWS = 512
VMEM_LIMIT = 56 * 1024 * 1024


def _mm(a, b):
    return jnp.dot(a.astype(BF16), b.astype(BF16), preferred_element_type=F32)


def _mm_nt(a, b):
    return lax.dot_general(a.astype(BF16), b.astype(BF16), (((1,), (1,)), ((), ())),
                           preferred_element_type=F32)


def _mm_tn(a, b):
    return lax.dot_general(a.astype(BF16), b.astype(BF16), (((0,), (0,)), ((), ())),
                           preferred_element_type=F32)


def _split_bf16(x, parts):
    out = []
    for _ in range(parts - 1):
        hi = x.astype(BF16)
        out.append(hi)
        x = x - hi.astype(F32)
    out.append(x.astype(BF16))
    return out


def _cumsum_rows(tri, x, parts):
    return sum(jnp.dot(tri, part, preferred_element_type=F32) for part in _split_bf16(x, parts))


def _head_sum(x, ones_bd):
    return jnp.dot(x.astype(BF16), ones_bd, preferred_element_type=F32)


def _rmsnorm(x, g):
    ms = jnp.mean(x * x, axis=-1, keepdims=True)
    return x * lax.rsqrt(ms + RMS_EPS) * g


def _silu(x):
    return x * jax.nn.sigmoid(x)


def _iota2(shape, axis):
    return lax.broadcasted_iota(jnp.int32, shape, axis)


def _causal_conv(x, prev_ref, w, b, rows):
    xe = jnp.concatenate([prev_ref[...], x], axis=0)
    prev_ref[...] = x[rows - CONV_HALO:, :]
    x1 = pltpu.roll(xe, 1, 0)
    acc = xe * w[3:4, :] + x1 * w[2:3, :] + pltpu.roll(xe * w[1:2, :] + x1 * w[0:1, :], 2, 0)
    return acc[CONV_HALO:, :] + b


def _ffn_body(h_ref, g_ref, wg_ref, wu_ref, wd_ref, fg_ref, o_ref, *, final):
    h = h_ref[...]
    hn = _rmsnorm(h, g_ref[...]).astype(BF16)
    gate = jnp.dot(hn, wg_ref[...], preferred_element_type=F32)
    up = jnp.dot(hn, wu_ref[...], preferred_element_type=F32)
    act = (_silu(gate) * up).astype(BF16)
    out = h + jnp.dot(act, wd_ref[...], preferred_element_type=F32)
    if final:
        out = _rmsnorm(out, fg_ref[...])
    o_ref[...] = out


def _resident(shape):
    nd = len(shape)
    return pl.BlockSpec(shape, lambda *_: (0,) * nd, pipeline_mode=pl.Buffered(1))


def _ffn(h2d, g, wg, wu, wd, fg, final):
    t = h2d.shape[0]
    rows = min(FFN_ROWS, t)
    row_spec = pl.BlockSpec((rows, D_MODEL), lambda i: (i, 0))
    return pl.pallas_call(
        functools.partial(_ffn_body, final=final),
        grid=(t // rows,),
        in_specs=[row_spec, _resident((1, D_MODEL)), _resident((D_MODEL, D_FF)), _resident((D_MODEL, D_FF)),
                  _resident((D_FF, D_MODEL)), _resident((1, D_MODEL))],
        out_specs=row_spec,
        out_shape=jax.ShapeDtypeStruct((t, D_MODEL), F32),
        compiler_params=pltpu.CompilerParams(dimension_semantics=("arbitrary",), vmem_limit_bytes=VMEM_LIMIT,
                                             allow_input_fusion=[False, False, True, True, True, False]),
        name="ffn_final" if final else "ffn",
    )(h2d, g, wg, wu, wd, fg)


def _pool_mixer(u, prev_ref, poolw_bd, pscale, first_pos, rows):
    ue = jnp.concatenate([prev_ref[...], u], axis=0)
    prev_ref[...] = u[rows - POOL_HALO:, :]
    sums = []
    acc = ue
    shift = 1
    for _ in POOL_WINDOWS:
        acc = acc + pltpu.roll(acc, shift, 0)
        sums.append(acc[POOL_HALO:, :])
        shift *= 2
    pos = (first_pos + 1 + _iota2((rows, 1), 0)).astype(F32)
    lane = _iota2((1, POOL_WIDTH), 1)
    pooled = sums[-1] * (1.0 / jnp.minimum(pos, float(POOL_WINDOWS[-1])))
    for gi in range(len(POOL_WINDOWS) - 2, -1, -1):
        cand = sums[gi] * (1.0 / jnp.minimum(pos, float(POOL_WINDOWS[gi])))
        pooled = jnp.where(lane < (gi + 1) * POOL_GROUP_DIM, cand, pooled)
    return _mm(pooled - u, poolw_bd) * pscale


def _ssd_chunk(c, refs, consts):
    xc_s, da_s, b_s, c_s, y_s, state_ref = refs
    tri, causal, lane_lo = consts
    L = SSD_CHUNK
    rows = pl.ds(pl.multiple_of(c * L, L), L)
    cum = _cumsum_rows(tri, da_s[rows, :], 2)
    xc = xc_s[rows, :]
    bmat = b_s[rows, :]
    cmat = c_s[rows, :]
    last = cum[L - 1:L, :]
    ecum = jnp.exp(cum)
    to_end = jnp.exp(last - cum)
    chunk_decay = jnp.exp(last)
    for g in range(SSD_GROUPS):
        gs = slice(g * SSD_GROUP_WIDTH, (g + 1) * SSD_GROUP_WIDTH)
        ns = slice(g * SSD_STATE, (g + 1) * SSD_STATE)
        bg, cg = bmat[:, ns], cmat[:, ns]
        cb = _mm_nt(cg, bg)
        hprev = state_ref[g]
        y_g = _mm(cg, hprev) * ecum[:, gs]
        ypairs = []
        for k in range(SSD_GROUP_WIDTH // 128):
            ps = slice(g * SSD_GROUP_WIDTH + k * 128, g * SSD_GROUP_WIDTH + (k + 1) * 128)
            slab = cum[:, ps]
            swapped = pltpu.roll(slab, 64, 1)
            xpair = xc[:, ps]
            outs = []
            for col in (jnp.where(lane_lo, slab, swapped), jnp.where(lane_lo, swapped, slab)):
                seg = col - col.T
                decay = jnp.where(causal, jnp.exp(jnp.minimum(seg, 0.0)), 0.0)
                outs.append(_mm(cb * decay, xpair))
            ypairs.append(jnp.where(lane_lo, outs[0], outs[1]))
        y_s[rows, gs] = y_g + jnp.concatenate(ypairs, axis=1)
        state_ref[g] = hprev * chunk_decay[:, gs] + _mm_tn(bg, xc[:, gs] * to_end[:, gs])
    return 0


def _even_body(h_ref, ng_ref, win_ref, wout_ref, poolw_ref, pscale_ref, convw_ref, convb_ref, dtb_ref, alog_ref,
               dskip_ref, ssdg_ref, expand_ref, o_ref, upool_prev, xbc_prev, state_ref, xc_s, da_s, b_s, c_s, y_s, *,
               rows):
    s = pl.program_id(1)

    @pl.when(s == 0)
    def _():
        upool_prev[...] = jnp.zeros_like(upool_prev)
        xbc_prev[...] = jnp.zeros_like(xbc_prev)
        state_ref[...] = jnp.zeros_like(state_ref)

    h = h_ref[...]
    p = _mm(_rmsnorm(h, ng_ref[...]), win_ref[...])
    o1, o2, o3 = POOL_WIDTH, POOL_WIDTH + SSD_WIDTH, POOL_WIDTH + SSD_WIDTH + SSD_CONV_DIM

    y_pool = _pool_mixer(p[:, :o1], upool_prev, poolw_ref[...], pscale_ref[...], s * rows, rows)

    z = p[:, o1:o2]
    xbc = _silu(_causal_conv(p[:, o2:o3], xbc_prev, convw_ref[...], convb_ref[...], rows))
    xh = xbc[:, :SSD_WIDTH]
    dt_heads = jax.nn.softplus(p[:, o3:] + dtb_ref[...])
    dt = sum(jnp.dot(part, expand_ref[...], preferred_element_type=F32) for part in _split_bf16(dt_heads, 3))
    xc_s[...] = xh * dt
    da_s[...] = dt * (-jnp.exp(alog_ref[...]))
    b_s[...] = xbc[:, SSD_WIDTH:SSD_WIDTH + SSD_GROUPS * SSD_STATE]
    c_s[...] = xbc[:, SSD_WIDTH + SSD_GROUPS * SSD_STATE:]

    L = SSD_CHUNK
    r_i, c_i = _iota2((L, L), 0), _iota2((L, L), 1)
    consts = ((r_i >= c_i).astype(BF16), r_i >= c_i, _iota2((1, 128), 1) < 64)
    refs = (xc_s, da_s, b_s, c_s, y_s, state_ref)
    lax.fori_loop(0, rows // L, lambda c, carry: _ssd_chunk(c, refs, consts), 0, unroll=4)

    y = (y_s[...] + dskip_ref[...] * xh) * _silu(z)
    normed = []
    for g in range(SSD_GROUPS):
        yy = y[:, g * SSD_GROUP_WIDTH:(g + 1) * SSD_GROUP_WIDTH]
        normed.append(yy * lax.rsqrt(jnp.mean(yy * yy, axis=-1, keepdims=True) + RMS_EPS))
    y_ssd = jnp.concatenate(normed, axis=1) * ssdg_ref[...]

    wout = wout_ref[...]
    o_ref[...] = h + _mm(y_pool, wout[:POOL_WIDTH, :]) + _mm(y_ssd, wout[POOL_WIDTH:, :])


def _seq_call(body, name, h, consts, scratch, rows):
    bsz, seq, _ = h.shape
    tile = pl.BlockSpec((pl.Squeezed(), rows, D_MODEL), lambda b, s: (b, s, 0))
    return pl.pallas_call(
        functools.partial(body, rows=rows),
        grid=(bsz, seq // rows),
        in_specs=[tile] + [_resident(c.shape) for c in consts],
        out_specs=tile,
        out_shape=jax.ShapeDtypeStruct(h.shape, F32),
        scratch_shapes=scratch,
        compiler_params=pltpu.CompilerParams(dimension_semantics=("arbitrary", "arbitrary"),
                                             vmem_limit_bytes=VMEM_LIMIT),
        name=name,
    )(h, *consts)


def _block_diag(w):
    g, c, _ = w.shape
    eye = jnp.eye(g, dtype=w.dtype)
    return (eye[:, None, :, None] * w[:, :, None, :]).reshape(g * c, g * c)


def _row(v):
    return v.reshape(1, -1).astype(F32)


def _even_layer(h, ng, w_in, w_out, pool_w, pool_scale, conv_w, conv_b, dt_bias, a_log, d_skip, norm_g, rows):
    o3 = POOL_WIDTH + SSD_WIDTH + SSD_CONV_DIM
    rep = lambda v: jnp.repeat(v, SSD_HEAD_DIM, axis=-1)
    w_in_p = jnp.concatenate([w_in[:, :o3], _pad_cols(w_in[:, o3:], SSD_DT_PAD)], axis=1).astype(BF16)
    head_of_channel = jnp.arange(SSD_WIDTH) // SSD_HEAD_DIM
    expand = (jnp.arange(SSD_DT_PAD)[:, None] == head_of_channel[None, :]).astype(BF16)
    consts = [_row(ng), w_in_p, w_out.astype(BF16), _block_diag(pool_w).astype(BF16), _row(pool_scale),
              conv_w.astype(F32), _row(conv_b), _pad_cols(_row(dt_bias), SSD_DT_PAD), _row(rep(a_log)),
              _row(rep(d_skip)), _row(norm_g), expand]
    scratch = [pltpu.VMEM((POOL_HALO, POOL_WIDTH), F32), pltpu.VMEM((CONV_HALO, SSD_CONV_DIM), F32),
               pltpu.VMEM((SSD_GROUPS, SSD_STATE, SSD_GROUP_WIDTH), F32),
               pltpu.VMEM((rows, SSD_WIDTH), F32), pltpu.VMEM((rows, SSD_WIDTH), F32),
               pltpu.VMEM((rows, SSD_GROUPS * SSD_STATE), F32), pltpu.VMEM((rows, SSD_GROUPS * SSD_STATE), F32),
               pltpu.VMEM((rows, SSD_WIDTH), F32)]
    return _seq_call(_even_body, "even_mixer", h, consts, scratch, rows)


def _stack_heads(x, lane_lo):
    return jnp.concatenate([jnp.where(lane_lo, x, 0.0), jnp.where(lane_lo, 0.0, x)], axis=0)


def _chunk_dot(a, b, ca, cb):
    return lax.dot_general(a.astype(BF16), b.astype(BF16), (((ca,), (cb,)), ((), ())), preferred_element_type=F32)


def _rwkv_prepare_chunk(c, refs, consts):
    r_s, k_s, v_s, kk_s, a_s, ld_s, rhat_s, yv_s, mt_s, gt_s, dec_s = refs
    tri, strict, incl, lane_lo, level_masks, eye = consts
    L = RWKV_CHUNK
    ops = []
    for j in range(RWKV_PREP_CHUNKS):
        cj = c * RWKV_PREP_CHUNKS + j
        rows = pl.ds(pl.multiple_of(cj * L, L), L)
        ld = ld_s[rows, :]
        cum = _cumsum_rows(tri, ld, 3)
        last = cum[L - 1:L, :]
        e_neg, e_end = jnp.exp(-cum), jnp.exp(last - cum)
        kk = kk_s[rows, :]
        kka = kk * a_s[rows, :]
        k = k_s[rows, :]
        ops.append(dict(a=-kk * jnp.exp(cum - ld), r=r_s[rows, :] * jnp.exp(cum), b=kka * e_neg, k=k * e_neg,
                        be=kka * e_end, ke=k * e_end, v=v_s[rows, :]))
        dec_s[pl.ds(pl.multiple_of(cj * 8, 8), 8), :] = jnp.broadcast_to(jnp.exp(last), (8, RWKV_WIDTH))

    items = [(j, q) for j in range(RWKV_PREP_CHUNKS) for q in range(RWKV_PAIRS)]
    P = range(len(items))
    st = lambda name, i: _stack_heads(ops[items[i][0]][name][:, items[i][1] * 128:(items[i][1] + 1) * 128], lane_lo)
    a2 = [st("a", i) for i in P]
    r2 = [st("r", i) for i in P]
    bk2 = [jnp.concatenate([st("b", i), st("k", i)], axis=0) for i in P]
    v2 = [st("v", i) for i in P]
    n_a = [jnp.where(strict, _chunk_dot(a2[i], bk2[i], 1, 1), 0.0) for i in P]
    n_r = [jnp.where(incl, _chunk_dot(r2[i], bk2[i], 1, 1), 0.0) for i in P]
    n_ab = [n_a[i][:, :2 * L] for i in P]

    x = [eye + jnp.where(level_masks[0], n_ab[i], 0.0) for i in P]
    for m in level_masks[1:]:
        t = [_chunk_dot(jnp.where(m, n_ab[i], 0.0), x[i], 1, 0) for i in P]
        x = [x[i] + _chunk_dot(x[i], t[i], 1, 0) for i in P]

    akv = [_chunk_dot(n_a[i][:, 2 * L:], v2[i], 1, 0) for i in P]
    au = [_chunk_dot(x[i], jnp.concatenate([a2[i], akv[i]], axis=1), 1, 0) for i in P]
    ru = [_chunk_dot(n_r[i][:, :2 * L], au[i], 1, 0) for i in P]
    rkv = [_chunk_dot(n_r[i][:, 2 * L:], v2[i], 1, 0) for i in P]
    mg = [_chunk_dot(au[i], st("be", i), 0, 0) for i in P]
    vk = [_chunk_dot(v2[i], st("ke", i), 0, 0) for i in P]
    for i in P:
        idx = (c * RWKV_PREP_CHUNKS + items[i][0]) * RWKV_PAIRS + items[i][1]
        rhat_s[idx] = (r2[i] + ru[i][:, :2 * L]).astype(BF16)
        yv = ru[i][:, 2 * L:] + rkv[i]
        yv_s[idx] = yv[:L, :] + yv[L:, :]
        mt_s[idx] = mg[i][:2 * L, :].astype(BF16)
        gt_s[idx] = mg[i][2 * L:, :] + vk[i]
    return 0


def _rwkv_scan_chunk(c, refs):
    rhat_s, yv_s, mt_s, gt_s, dec_s, y_s, state_ref = refs
    L = RWKV_CHUNK
    rows = pl.ds(pl.multiple_of(c * L, L), L)
    decay = dec_s[pl.ds(pl.multiple_of(c * 8, 8), 8), :][:1, :]
    for q in range(RWKV_PAIRS):
        idx = c * RWKV_PAIRS + q
        sl = slice(q * 128, (q + 1) * 128)
        state = state_ref[q]
        y2 = _chunk_dot(rhat_s[idx], state, 1, 1)
        y_s[rows, sl] = y2[:L, :] + y2[L:, :] + yv_s[idx]
        state_ref[q] = state * decay[:, sl] + _chunk_dot(state, mt_s[idx], 1, 0) + gt_s[idx]
    return 0


def _lru_scan(a, b, carry_ref, rows):
    groups, width = rows // 8, a.shape[1]
    a3, b3 = a.reshape(groups, 8, width), b.reshape(groups, 8, width)
    sub = _iota2((1, 8, 1), 1)
    d = 1
    while d < 8:
        valid = sub >= d
        b3 = jnp.where(valid, a3 * pltpu.roll(b3, d, 1) + b3, b3)
        a3 = jnp.where(valid, a3 * pltpu.roll(a3, d, 1), a3)
        d *= 2
    h_in = carry_ref[CONV_HALO - 1:CONV_HALO, :]
    blocks = []
    for g in range(groups):
        blocks.append(b3[g] + a3[g] * h_in)
        h_in = blocks[-1][7:8, :]
    carry_ref[...] = blocks[-1]
    return jnp.concatenate(blocks, axis=0)


def _odd_body(h_ref, ng_ref, win_ref, wout_ref, mu_ref, w0_ref, wup_ref, a0_ref, aup_ref, gup_ref, kk_ref, ka_ref,
              rk_ref, lng_ref, lnb_ref, lcw_ref, lcb_ref, wax_ref, bax_ref, lam_ref, ones_ref, o_ref,
              shift_prev, lru_prev, lru_carry, state_ref, r_s, k_s, v_s, kk_s, a_s, ld_s, y_s,
              rhat_s, yv_s, mt_s, gt_s, dec_s, *, rows):
    s = pl.program_id(1)

    @pl.when(s == 0)
    def _():
        shift_prev[...] = jnp.zeros_like(shift_prev)
        lru_prev[...] = jnp.zeros_like(lru_prev)
        lru_carry[...] = jnp.zeros_like(lru_carry)
        state_ref[...] = jnp.zeros_like(state_ref)

    h = h_ref[...]
    p = _mm(_rmsnorm(h, ng_ref[...]), win_ref[...])

    pr = p[:, :RWKV_SHIFT_WIDTH]
    prev = jnp.where(_iota2((rows, 1), 0) == 0, shift_prev[CONV_HALO - 1:CONV_HALO, :], pltpu.roll(pr, 1, 0))
    shift_prev[...] = pr[rows - CONV_HALO:, :]
    ps = pr + mu_ref[...] * (prev - pr)
    W = RWKV_WIDTH
    r, k, v = ps[:, :W], ps[:, W:2 * W], ps[:, 2 * W:3 * W]
    o = 3 * W
    wd, ad, gd = (ps[:, o + i * RWKV_LORA_PAD:o + (i + 1) * RWKV_LORA_PAD] for i in range(3))
    w_log = -jax.nn.softplus(-(w0_ref[...] + _mm(jnp.tanh(wd), wup_ref[...]))) - RWKV_DECAY_OFFSET
    a = jax.nn.sigmoid(a0_ref[...] + _mm(ad, aup_ref[...]))
    gate = _mm(jax.nn.sigmoid(gd), gup_ref[...])
    ones_bd = ones_ref[...]
    kk = k * kk_ref[...]
    kk = kk * lax.rsqrt(_head_sum(kk * kk, ones_bd) + 1e-12)
    k = k * (1.0 + (a - 1.0) * ka_ref[...])
    r_s[...], k_s[...], v_s[...], kk_s[...], a_s[...] = r, k, v, kk, a
    ld_s[...] = -jnp.exp(w_log)

    L = RWKV_CHUNK
    r_i, c_i = _iota2((2 * L, 2 * L), 0), _iota2((2 * L, 2 * L), 1)
    level_masks = []
    b = 1
    while b < L:
        same_block = (r_i & -(2 * b)) == (c_i & -(2 * b))
        level_masks.append(same_block & ((r_i & b) != 0) & ((c_i & b) == 0))
        b *= 2
    t_i, s_i = _iota2((L, L), 0), _iota2((L, L), 1)
    r_w, c_w = _iota2((2 * L, 4 * L), 0), _iota2((2 * L, 4 * L), 1) & (2 * L - 1)
    consts = ((t_i >= s_i).astype(BF16), r_w > c_w, r_w >= c_w, _iota2((1, 128), 1) < 64, level_masks,
              (r_i == c_i).astype(F32))
    prep_refs = (r_s, k_s, v_s, kk_s, a_s, ld_s, rhat_s, yv_s, mt_s, gt_s, dec_s)
    lax.fori_loop(0, rows // (L * RWKV_PREP_CHUNKS), lambda c, carry: _rwkv_prepare_chunk(c, prep_refs, consts), 0)
    scan_refs = (rhat_s, yv_s, mt_s, gt_s, dec_s, y_s, state_ref)
    lax.fori_loop(0, rows // L, lambda c, carry: _rwkv_scan_chunk(c, scan_refs), 0, unroll=4)

    y = y_s[...]
    inv_n = 1.0 / RWKV_HEAD_DIM
    mean = _head_sum(y, ones_bd) * inv_n
    yc = y - mean
    var = _head_sum(yc * yc, ones_bd) * inv_n
    y = yc * lax.rsqrt(var + RWKV_GN_EPS) * lng_ref[...] + lnb_ref[...]
    y = y + _head_sum(r * k * rk_ref[...], ones_bd) * v
    y_rwkv = y * gate

    gate_l = p[:, RWKV_SHIFT_WIDTH:RWKV_SHIFT_WIDTH + LRU_WIDTH]
    xb = _causal_conv(p[:, RWKV_SHIFT_WIDTH + LRU_WIDTH:], lru_prev, lcw_ref[...], lcb_ref[...], rows)
    gates = jax.nn.sigmoid(_mm(xb, wax_ref[...]) + bax_ref[...])
    log_a = -LRU_C * gates[:, :LRU_WIDTH] * jax.nn.softplus(-lam_ref[...])
    a_l = jnp.exp(log_a)
    b_l = jnp.sqrt(-jnp.tanh(log_a) * (a_l * a_l + 1.0)) * gates[:, LRU_WIDTH:] * xb
    y_lru = _lru_scan(a_l, b_l, lru_carry, rows) * jax.nn.gelu(gate_l)

    wout = wout_ref[...]
    o_ref[...] = h + _mm(y_rwkv, wout[:RWKV_WIDTH, :]) + _mm(y_lru, wout[RWKV_WIDTH:, :])


def _pad_rows(w, n):
    return jnp.pad(w, ((0, n - w.shape[0]), (0, 0)))


def _pad_cols(w, n):
    return jnp.pad(w, ((0, 0), (0, n - w.shape[-1])))


def _odd_layer(h, ng, w_in, w_out, mu, w0, w_up, a0, a_up, g_up, k_k, k_a, r_k, ln_g, ln_b, lconv_w, lconv_b,
               wa, ba, wx, bx, lam, rows):
    W, P = RWKV_WIDTH, RWKV_LORA_PAD
    o = 3 * W
    dr, ir = w_up.shape[0], a_up.shape[0]

    def pad_sections(m):
        return jnp.concatenate([m[..., :o], _pad_cols(m[..., o:o + dr], P), _pad_cols(m[..., o + dr:o + dr + ir], P),
                                m[..., o + dr + ir:]], axis=-1)

    w_in_p = pad_sections(w_in).astype(BF16)
    mu_p = pad_sections(mu.reshape(1, -1))
    ones_bd = _block_diag(jnp.ones((W // RWKV_HEAD_DIM, RWKV_HEAD_DIM, RWKV_HEAD_DIM), BF16))
    wax = jnp.concatenate([_block_diag(wa), _block_diag(wx)], axis=1).astype(BF16)
    consts = [_row(ng), w_in_p, w_out.astype(BF16), mu_p.astype(F32), _row(w0), _pad_rows(w_up, P).astype(BF16),
              _row(a0), _pad_rows(a_up, P).astype(BF16), g_up.astype(BF16), _row(k_k), _row(k_a), _row(r_k),
              _row(ln_g), _row(ln_b), lconv_w.astype(F32), _row(lconv_b), wax,
              _row(jnp.concatenate([ba, bx])), _row(lam), ones_bd]
    scratch = [pltpu.VMEM((CONV_HALO, RWKV_SHIFT_WIDTH), F32), pltpu.VMEM((CONV_HALO, LRU_WIDTH), F32),
               pltpu.VMEM((CONV_HALO, LRU_WIDTH), F32), pltpu.VMEM((RWKV_PAIRS, 128, 128), F32)]
    scratch += [pltpu.VMEM((rows, W), F32) for _ in range(7)]
    n_pc = (rows // RWKV_CHUNK) * RWKV_PAIRS
    scratch += [pltpu.VMEM((n_pc, 128, 128), BF16), pltpu.VMEM((n_pc, RWKV_CHUNK, 128), F32),
                pltpu.VMEM((n_pc, 128, 128), BF16), pltpu.VMEM((n_pc, 128, 128), F32),
                pltpu.VMEM((rows // RWKV_CHUNK * 8, W), F32)]
    return _seq_call(_odd_body, "odd_mixer", h, consts, scratch, rows)


def kernel(x, mix_norm_g, ffn_norm_g, ffn_w_gate, ffn_w_up, ffn_w_down, final_norm_g, ev_w_in, ev_w_out, pool_w,
           pool_scale, ssd_conv_w, ssd_conv_b, ssd_dt_bias, ssd_a_log, ssd_d, ssd_norm_g, od_w_in, od_w_out,
           rwkv_mu, rwkv_w0, rwkv_w_up, rwkv_a0, rwkv_a_up, rwkv_g_up, rwkv_k_k, rwkv_k_a, rwkv_r_k, rwkv_ln_g,
           rwkv_ln_b, lru_conv_w, lru_conv_b, lru_wa, lru_ba, lru_wx, lru_bx, lru_lambda):
    bsz, seq, d = x.shape
    assert d == D_MODEL and x.dtype == F32
    rows = min(SEQ_TILE, seq)
    assert seq % rows == 0 and rows % SSD_CHUNK == 0
    depth = mix_norm_g.shape[0]
    h = x
    for layer in range(depth):
        i = layer // 2
        if layer % 2 == 0:
            h = _even_layer(h, mix_norm_g[layer], ev_w_in[i], ev_w_out[i], pool_w[i], pool_scale[i], ssd_conv_w[i],
                            ssd_conv_b[i], ssd_dt_bias[i], ssd_a_log[i], ssd_d[i], ssd_norm_g[i], rows)
        else:
            h = _odd_layer(h, mix_norm_g[layer], od_w_in[i], od_w_out[i], rwkv_mu[i], rwkv_w0[i], rwkv_w_up[i],
                           rwkv_a0[i], rwkv_a_up[i], rwkv_g_up[i], rwkv_k_k[i], rwkv_k_a[i],
                           rwkv_r_k[i].reshape(-1), rwkv_ln_g[i], rwkv_ln_b[i], lru_conv_w[i], lru_conv_b[i],
                           lru_wa[i], lru_ba[i], lru_wx[i], lru_bx[i], lru_lambda[i], rows)
        final = layer == depth - 1
        h = _ffn(h.reshape(bsz * seq, d), _row(ffn_norm_g[layer]), ffn_w_gate[layer].astype(BF16),
                 ffn_w_up[layer].astype(BF16), ffn_w_down[layer].astype(BF16), _row(final_norm_g),
                 final).reshape(bsz, seq, d)
    return h
```

```python
import functools
import math

import jax
import jax.numpy as jnp
from jax import lax
from jax.experimental import pallas as pl
from jax.experimental.pallas import tpu as pltpu

F32 = jnp.float32
BF16 = jnp.bfloat16

D_MODEL = 1024
D_FF = 2816
RMS_EPS = 1e-6

POOL_WIDTH = 256
POOL_WINDOWS = (2, 4, 8, 16)
POOL_GROUP_DIM = 64
POOL_HALO = 16

SSD_WIDTH = 768
SSD_HEADS = 12
SSD_HEAD_DIM = 64
SSD_GROUPS = 2
SSD_GROUP_WIDTH = SSD_WIDTH // SSD_GROUPS
SSD_STATE = 128
SSD_CHUNK = 128
SSD_CONV_DIM = 1280
CONV_K = 4
CONV_HALO = 8

RWKV_WIDTH = 512
RWKV_HEAD_DIM = 64
RWKV_PAIRS = RWKV_WIDTH // 128
RWKV_CHUNK = 64
RWKV_PREP_CHUNKS = 4
RWKV_LORA_PAD = 128
RWKV_SHIFT_WIDTH = 3 * RWKV_WIDTH + 3 * RWKV_LORA_PAD
RWKV_GN_EPS = 64e-5
RWKV_DECAY_OFFSET = 0.5
LRU_WIDTH = 512
LRU_C = 8.0

SSD_DT_PAD = 128
ODD_IN_PAD = RWKV_SHIFT_WIDTH + 2 * LRU_WIDTH

SEQ_TILE = 512
SUB_ROWS = 256
FFN_ROWS = 512
VMEM_LIMIT = 56 * 1024 * 1024


def _mm(a, b):
    return jnp.dot(a.astype(BF16), b.astype(BF16), preferred_element_type=F32)


def _mm_nt(a, b):
    return lax.dot_general(a.astype(BF16), b.astype(BF16), (((1,), (1,)), ((), ())),
                           preferred_element_type=F32)


def _mm_tn(a, b):
    return lax.dot_general(a.astype(BF16), b.astype(BF16), (((0,), (0,)), ((), ())),
                           preferred_element_type=F32)


def _split_bf16(x, parts):
    out = []
    for _ in range(parts - 1):
        hi = x.astype(BF16)
        out.append(hi)
        x = x - hi.astype(F32)
    out.append(x.astype(BF16))
    return out


def _cumsum_rows(tri, x, parts):
    return sum(jnp.dot(tri, part, preferred_element_type=F32) for part in _split_bf16(x, parts))


def _head_sum(x, ones_bd):
    return jnp.dot(x.astype(BF16), ones_bd, preferred_element_type=F32)


def _rmsnorm(x, g):
    ms = jnp.mean(x * x, axis=-1, keepdims=True)
    return x * lax.rsqrt(ms + RMS_EPS) * g


def _silu(x):
    return x * jax.nn.sigmoid(x)


def _iota2(shape, axis):
    return lax.broadcasted_iota(jnp.int32, shape, axis)


def _causal_conv(x, prev_ref, w, b, rows):
    xe = jnp.concatenate([prev_ref[...], x], axis=0)
    prev_ref[...] = x[rows - CONV_HALO:, :]
    x1 = pltpu.roll(xe, 1, 0)
    acc = xe * w[3:4, :] + x1 * w[2:3, :] + pltpu.roll(xe * w[1:2, :] + x1 * w[0:1, :], 2, 0)
    return acc[CONV_HALO:, :] + b


def _ffn_body(h_ref, g_ref, wg_ref, wu_ref, wd_ref, fg_ref, o_ref, *, final):
    h = h_ref[...]
    hn = _rmsnorm(h, g_ref[...]).astype(BF16)
    gate = jnp.dot(hn, wg_ref[...], preferred_element_type=F32)
    up = jnp.dot(hn, wu_ref[...], preferred_element_type=F32)
    act = (_silu(gate) * up).astype(BF16)
    out = h + jnp.dot(act, wd_ref[...], preferred_element_type=F32)
    if final:
        out = _rmsnorm(out, fg_ref[...])
    o_ref[...] = out


def _resident(shape):
    nd = len(shape)
    return pl.BlockSpec(shape, lambda *_: (0,) * nd, pipeline_mode=pl.Buffered(1))


def _ffn(h2d, g, wg, wu, wd, fg, final):
    t = h2d.shape[0]
    rows = min(FFN_ROWS, t)
    row_spec = pl.BlockSpec((rows, D_MODEL), lambda i: (i, 0))
    return pl.pallas_call(
        functools.partial(_ffn_body, final=final),
        grid=(t // rows,),
        in_specs=[row_spec, _resident((1, D_MODEL)), _resident((D_MODEL, D_FF)), _resident((D_MODEL, D_FF)),
                  _resident((D_FF, D_MODEL)), _resident((1, D_MODEL))],
        out_specs=row_spec,
        out_shape=jax.ShapeDtypeStruct((t, D_MODEL), F32),
        compiler_params=pltpu.CompilerParams(dimension_semantics=("arbitrary",), vmem_limit_bytes=VMEM_LIMIT),
        name="ffn_final" if final else "ffn",
    )(h2d, g, wg, wu, wd, fg)


def _pool_mixer(u, prev_ref, poolw_bd, pscale, first_pos, rows):
    ue = jnp.concatenate([prev_ref[...], u], axis=0)
    prev_ref[...] = u[rows - POOL_HALO:, :]
    sums = []
    acc = ue
    shift = 1
    for _ in POOL_WINDOWS:
        acc = acc + pltpu.roll(acc, shift, 0)
        sums.append(acc[POOL_HALO:, :])
        shift *= 2
    pos = (first_pos + 1 + _iota2((rows, 1), 0)).astype(F32)
    lane = _iota2((1, POOL_WIDTH), 1)
    pooled = sums[-1] * (1.0 / jnp.minimum(pos, float(POOL_WINDOWS[-1])))
    for gi in range(len(POOL_WINDOWS) - 2, -1, -1):
        cand = sums[gi] * (1.0 / jnp.minimum(pos, float(POOL_WINDOWS[gi])))
        pooled = jnp.where(lane < (gi + 1) * POOL_GROUP_DIM, cand, pooled)
    return _mm(pooled - u, poolw_bd) * pscale


def _ssd_chunk(c, refs, consts):
    xc_s, da_s, b_s, c_s, y_s, state_ref = refs
    tri, causal, lane_lo = consts
    L = SSD_CHUNK
    rows = pl.ds(c * L, L)
    cum = _cumsum_rows(tri, da_s[rows, :], 2)
    xc = xc_s[rows, :]
    bmat = b_s[rows, :]
    cmat = c_s[rows, :]
    last = cum[L - 1:L, :]
    ecum = jnp.exp(cum)
    to_end = jnp.exp(last - cum)
    chunk_decay = jnp.exp(last)
    for g in range(SSD_GROUPS):
        gs = slice(g * SSD_GROUP_WIDTH, (g + 1) * SSD_GROUP_WIDTH)
        ns = slice(g * SSD_STATE, (g + 1) * SSD_STATE)
        bg, cg = bmat[:, ns], cmat[:, ns]
        cb = _mm_nt(cg, bg)
        hprev = state_ref[g]
        y_g = _mm(cg, hprev) * ecum[:, gs]
        ypairs = []
        for k in range(SSD_GROUP_WIDTH // 128):
            ps = slice(g * SSD_GROUP_WIDTH + k * 128, g * SSD_GROUP_WIDTH + (k + 1) * 128)
            slab = cum[:, ps]
            swapped = pltpu.roll(slab, 64, 1)
            xpair = xc[:, ps]
            outs = []
            for col in (jnp.where(lane_lo, slab, swapped), jnp.where(lane_lo, swapped, slab)):
                seg = col - col.T
                decay = jnp.where(causal, jnp.exp(jnp.minimum(seg, 0.0)), 0.0)
                outs.append(_mm(cb * decay, xpair))
            ypairs.append(jnp.where(lane_lo, outs[0], outs[1]))
        y_s[rows, gs] = y_g + jnp.concatenate(ypairs, axis=1)
        state_ref[g] = hprev * chunk_decay[:, gs] + _mm_tn(bg, xc[:, gs] * to_end[:, gs])
    return 0


def _even_body(h_ref, ng_ref, win_ref, wout_ref, poolw_ref, pscale_ref, convw_ref, convb_ref, dtb_ref, alog_ref,
               dskip_ref, ssdg_ref, expand_ref, o_ref, upool_prev, xbc_prev, state_ref, xc_s, da_s, b_s, c_s, y_s, *,
               rows):
    s = pl.program_id(1)

    @pl.when(s == 0)
    def _():
        upool_prev[...] = jnp.zeros_like(upool_prev)
        xbc_prev[...] = jnp.zeros_like(xbc_prev)
        state_ref[...] = jnp.zeros_like(state_ref)

    o1, o2, o3 = POOL_WIDTH, POOL_WIDTH + SSD_WIDTH, POOL_WIDTH + SSD_WIDTH + SSD_CONV_DIM
    sub = min(SUB_ROWS, rows)
    subs = [slice(j * sub, (j + 1) * sub) for j in range(rows // sub)]

    L = SSD_CHUNK
    r_i, c_i = _iota2((L, L), 0), _iota2((L, L), 1)
    consts = ((r_i >= c_i).astype(BF16), r_i >= c_i, _iota2((1, 128), 1) < 64)
    refs = (xc_s, da_s, b_s, c_s, y_s, state_ref)
    wout = wout_ref[...]

    def front(j, rs):
        p = _mm(_rmsnorm(h_ref[rs, :], ng_ref[...]), win_ref[...])
        y_pool = _pool_mixer(p[:, :o1], upool_prev, poolw_ref[...], pscale_ref[...], s * rows + j * sub, sub)
        xbc = _silu(_causal_conv(p[:, o2:o3], xbc_prev, convw_ref[...], convb_ref[...], sub))
        xh = xbc[:, :SSD_WIDTH]
        dt_heads = jax.nn.softplus(p[:, o3:] + dtb_ref[...])
        dt = sum(jnp.dot(part, expand_ref[...], preferred_element_type=F32) for part in _split_bf16(dt_heads, 3))
        xc_s[rs, :] = xh * dt
        da_s[rs, :] = dt * (-jnp.exp(alog_ref[...]))
        b_s[rs, :] = xbc[:, SSD_WIDTH:SSD_WIDTH + SSD_GROUPS * SSD_STATE]
        c_s[rs, :] = xbc[:, SSD_WIDTH + SSD_GROUPS * SSD_STATE:]
        return y_pool, xh, p[:, o1:o2]

    def scan(j):
        for c in range(j * (sub // L), (j + 1) * (sub // L)):
            _ssd_chunk(c, refs, consts)

    def tail(rs, y_pool, xh, z):
        y = (y_s[rs, :] + dskip_ref[...] * xh) * _silu(z)
        normed = []
        for g in range(SSD_GROUPS):
            yy = y[:, g * SSD_GROUP_WIDTH:(g + 1) * SSD_GROUP_WIDTH]
            normed.append(yy * lax.rsqrt(jnp.mean(yy * yy, axis=-1, keepdims=True) + RMS_EPS))
        y_ssd = jnp.concatenate(normed, axis=1) * ssdg_ref[...]
        o_ref[rs, :] = h_ref[rs, :] + _mm(y_pool, wout[:POOL_WIDTH, :]) + _mm(y_ssd, wout[POOL_WIDTH:, :])

    fronts = [front(j, rs) for j, rs in enumerate(subs)]
    for j in range(len(subs)):
        scan(j)
    for rs, parts in zip(subs, fronts):
        tail(rs, *parts)


def _seq_call(body, name, h, consts, scratch, rows):
    bsz, seq, _ = h.shape
    tile = pl.BlockSpec((pl.Squeezed(), rows, D_MODEL), lambda b, s: (b, s, 0))
    return pl.pallas_call(
        functools.partial(body, rows=rows),
        grid=(bsz, seq // rows),
        in_specs=[tile] + [_resident(c.shape) for c in consts],
        out_specs=tile,
        out_shape=jax.ShapeDtypeStruct(h.shape, F32),
        scratch_shapes=scratch,
        compiler_params=pltpu.CompilerParams(dimension_semantics=("arbitrary", "arbitrary"),
                                             vmem_limit_bytes=VMEM_LIMIT),
        name=name,
    )(h, *consts)


def _block_diag(w):
    g, c, _ = w.shape
    eye = jnp.eye(g, dtype=w.dtype)
    return (eye[:, None, :, None] * w[:, :, None, :]).reshape(g * c, g * c)


def _row(v):
    return v.reshape(1, -1).astype(F32)


def _even_layer(h, ng, w_in, w_out, pool_w, pool_scale, conv_w, conv_b, dt_bias, a_log, d_skip, norm_g, rows):
    o3 = POOL_WIDTH + SSD_WIDTH + SSD_CONV_DIM
    rep = lambda v: jnp.repeat(v, SSD_HEAD_DIM, axis=-1)
    w_in_p = jnp.concatenate([w_in[:, :o3], _pad_cols(w_in[:, o3:], SSD_DT_PAD)], axis=1).astype(BF16)
    head_of_channel = jnp.arange(SSD_WIDTH) // SSD_HEAD_DIM
    expand = (jnp.arange(SSD_DT_PAD)[:, None] == head_of_channel[None, :]).astype(BF16)
    consts = [_row(ng), w_in_p, w_out.astype(BF16), _block_diag(pool_w).astype(BF16), _row(pool_scale),
              conv_w.astype(F32), _row(conv_b), _pad_cols(_row(dt_bias), SSD_DT_PAD), _row(rep(a_log)),
              _row(rep(d_skip)), _row(norm_g), expand]
    scratch = [pltpu.VMEM((POOL_HALO, POOL_WIDTH), F32), pltpu.VMEM((CONV_HALO, SSD_CONV_DIM), F32),
               pltpu.VMEM((SSD_GROUPS, SSD_STATE, SSD_GROUP_WIDTH), F32),
               pltpu.VMEM((rows, SSD_WIDTH), F32), pltpu.VMEM((rows, SSD_WIDTH), F32),
               pltpu.VMEM((rows, SSD_GROUPS * SSD_STATE), F32), pltpu.VMEM((rows, SSD_GROUPS * SSD_STATE), F32),
               pltpu.VMEM((rows, SSD_WIDTH), F32)]
    return _seq_call(_even_body, "even_mixer", h, consts, scratch, rows)


def _stack_heads(x, lane_lo):
    return jnp.concatenate([jnp.where(lane_lo, x, 0.0), jnp.where(lane_lo, 0.0, x)], axis=0)


def _chunk_dot(a, b, ca, cb):
    return lax.dot_general(a.astype(BF16), b.astype(BF16), (((ca,), (cb,)), ((), ())), preferred_element_type=F32)


def _rwkv_prepare_chunk(c, refs, consts):
    r_s, k_s, v_s, kk_s, a_s, ld_s, rhat_s, yv_s, mt_s, gt_s, dec_s = refs
    tri, strict, incl, lane_lo, level_masks, eye = consts
    L = RWKV_CHUNK
    ops = []
    for j in range(RWKV_PREP_CHUNKS):
        cj = c * RWKV_PREP_CHUNKS + j
        rows = pl.ds(pl.multiple_of(cj * L, L), L)
        ld = ld_s[rows, :]
        cum = _cumsum_rows(tri, ld, 3)
        last = cum[L - 1:L, :]
        e_neg, e_end = jnp.exp(-cum), jnp.exp(last - cum)
        kk = kk_s[rows, :]
        kka = kk * a_s[rows, :]
        k = k_s[rows, :]
        ops.append(dict(a=-kk * jnp.exp(cum - ld), r=r_s[rows, :] * jnp.exp(cum), b=kka * e_neg, k=k * e_neg,
                        be=kka * e_end, ke=k * e_end, v=v_s[rows, :]))
        dec_s[pl.ds(pl.multiple_of(cj * 8, 8), 8), :] = jnp.broadcast_to(jnp.exp(last), (8, RWKV_WIDTH))

    items = [(j, q) for j in range(RWKV_PREP_CHUNKS) for q in range(RWKV_PAIRS)]
    P = range(len(items))
    st = lambda name, i: _stack_heads(ops[items[i][0]][name][:, items[i][1] * 128:(items[i][1] + 1) * 128], lane_lo)
    a2 = [st("a", i) for i in P]
    r2 = [st("r", i) for i in P]
    bk2 = [jnp.concatenate([st("b", i), st("k", i)], axis=0) for i in P]
    v2 = [st("v", i) for i in P]
    n_a = [jnp.where(strict, _chunk_dot(a2[i], bk2[i], 1, 1), 0.0) for i in P]
    n_r = [jnp.where(incl, _chunk_dot(r2[i], bk2[i], 1, 1), 0.0) for i in P]
    n_ab = [n_a[i][:, :2 * L] for i in P]

    x = [eye + jnp.where(level_masks[0], n_ab[i], 0.0) for i in P]
    for m in level_masks[1:]:
        t = [_chunk_dot(jnp.where(m, n_ab[i], 0.0), x[i], 1, 0) for i in P]
        x = [x[i] + _chunk_dot(x[i], t[i], 1, 0) for i in P]

    akv = [_chunk_dot(n_a[i][:, 2 * L:], v2[i], 1, 0) for i in P]
    au = [_chunk_dot(x[i], jnp.concatenate([a2[i], akv[i]], axis=1), 1, 0) for i in P]
    ru = [_chunk_dot(n_r[i][:, :2 * L], au[i], 1, 0) for i in P]
    rkv = [_chunk_dot(n_r[i][:, 2 * L:], v2[i], 1, 0) for i in P]
    mg = [_chunk_dot(au[i], st("be", i), 0, 0) for i in P]
    vk = [_chunk_dot(v2[i], st("ke", i), 0, 0) for i in P]
    for i in P:
        idx = (c * RWKV_PREP_CHUNKS + items[i][0]) * RWKV_PAIRS + items[i][1]
        rhat_s[idx] = (r2[i] + ru[i][:, :2 * L]).astype(BF16)
        yv = ru[i][:, 2 * L:] + rkv[i]
        yv_s[idx] = yv[:L, :] + yv[L:, :]
        mt_s[idx] = mg[i][:2 * L, :].astype(BF16)
        gt_s[idx] = mg[i][2 * L:, :] + vk[i]
    return 0


def _rwkv_scan_chunk(c, refs):
    rhat_s, yv_s, mt_s, gt_s, dec_s, y_s, state_ref = refs
    L = RWKV_CHUNK
    rows = pl.ds(pl.multiple_of(c * L, L), L)
    decay = dec_s[pl.ds(pl.multiple_of(c * 8, 8), 8), :][:1, :]
    for q in range(RWKV_PAIRS):
        idx = c * RWKV_PAIRS + q
        sl = slice(q * 128, (q + 1) * 128)
        state = state_ref[q]
        y2 = _chunk_dot(rhat_s[idx], state, 1, 1)
        y_s[rows, sl] = y2[:L, :] + y2[L:, :] + yv_s[idx]
        state_ref[q] = state * decay[:, sl] + _chunk_dot(state, mt_s[idx], 1, 0) + gt_s[idx]
    return 0


def _lru_scan(a, b, carry_ref, rows):
    groups, width = rows // 8, a.shape[1]
    a3, b3 = a.reshape(groups, 8, width), b.reshape(groups, 8, width)
    sub = _iota2((1, 8, 1), 1)
    d = 1
    while d < 8:
        valid = sub >= d
        b3 = jnp.where(valid, a3 * pltpu.roll(b3, d, 1) + b3, b3)
        a3 = jnp.where(valid, a3 * pltpu.roll(a3, d, 1), a3)
        d *= 2
    h_in = carry_ref[CONV_HALO - 1:CONV_HALO, :]
    blocks = []
    for g in range(groups):
        blocks.append(b3[g] + a3[g] * h_in)
        h_in = blocks[-1][7:8, :]
    carry_ref[...] = blocks[-1]
    return jnp.concatenate(blocks, axis=0)


def _odd_body(h_ref, ng_ref, win_ref, wout_ref, mu_ref, w0_ref, wup_ref, a0_ref, aup_ref, gup_ref, kk_ref, ka_ref,
              rk_ref, lng_ref, lnb_ref, lcw_ref, lcb_ref, wax_ref, bax_ref, lam_ref, ones_ref, o_ref,
              shift_prev, lru_prev, lru_carry, state_ref, r_s, k_s, v_s, kk_s, a_s, ld_s, y_s,
              rhat_s, yv_s, mt_s, gt_s, dec_s, *, rows):
    s = pl.program_id(1)

    @pl.when(s == 0)
    def _():
        shift_prev[...] = jnp.zeros_like(shift_prev)
        lru_prev[...] = jnp.zeros_like(lru_prev)
        lru_carry[...] = jnp.zeros_like(lru_carry)
        state_ref[...] = jnp.zeros_like(state_ref)

    W = RWKV_WIDTH
    ones_bd = ones_ref[...]
    sub = rows
    subs = [slice(j * sub, (j + 1) * sub) for j in range(rows // sub)]

    def front(rs):
        p = _mm(_rmsnorm(h_ref[rs, :], ng_ref[...]), win_ref[...])
        pr = p[:, :RWKV_SHIFT_WIDTH]
        prev = jnp.where(_iota2((sub, 1), 0) == 0, shift_prev[CONV_HALO - 1:CONV_HALO, :], pltpu.roll(pr, 1, 0))
        shift_prev[...] = pr[sub - CONV_HALO:, :]
        ps = pr + mu_ref[...] * (prev - pr)
        r, k, v = ps[:, :W], ps[:, W:2 * W], ps[:, 2 * W:3 * W]
        o = 3 * W
        wd, ad, gd = (ps[:, o + i * RWKV_LORA_PAD:o + (i + 1) * RWKV_LORA_PAD] for i in range(3))
        log_decay = -math.exp(-RWKV_DECAY_OFFSET) * jax.nn.sigmoid(w0_ref[...] + _mm(jnp.tanh(wd), wup_ref[...]))
        a = jax.nn.sigmoid(a0_ref[...] + _mm(ad, aup_ref[...]))
        gate = _mm(jax.nn.sigmoid(gd), gup_ref[...])
        kk = k * kk_ref[...]
        kk = kk * lax.rsqrt(_head_sum(kk * kk, ones_bd) + 1e-12)
        k = k * (1.0 + (a - 1.0) * ka_ref[...])
        r_s[rs, :], k_s[rs, :], v_s[rs, :], kk_s[rs, :], a_s[rs, :] = r, k, v, kk, a
        ld_s[rs, :] = log_decay

        gate_l = p[:, RWKV_SHIFT_WIDTH:RWKV_SHIFT_WIDTH + LRU_WIDTH]
        xb = _causal_conv(p[:, RWKV_SHIFT_WIDTH + LRU_WIDTH:], lru_prev, lcw_ref[...], lcb_ref[...], sub)
        gates = jax.nn.sigmoid(_mm(xb, wax_ref[...]) + bax_ref[...])
        log_a = -LRU_C * gates[:, :LRU_WIDTH] * jax.nn.softplus(-lam_ref[...])
        a_l = jnp.exp(log_a)
        b_l = jnp.sqrt(-jnp.tanh(log_a) * (a_l * a_l + 1.0)) * gates[:, LRU_WIDTH:] * xb
        y_lru = _lru_scan(a_l, b_l, lru_carry, sub) * jax.nn.gelu(gate_l)
        return gate, y_lru

    fronts = [front(rs) for rs in subs]

    L = RWKV_CHUNK
    r_i, c_i = _iota2((2 * L, 2 * L), 0), _iota2((2 * L, 2 * L), 1)
    level_masks = []
    b = 1
    while b < L:
        same_block = (r_i & -(2 * b)) == (c_i & -(2 * b))
        level_masks.append(same_block & ((r_i & b) != 0) & ((c_i & b) == 0))
        b *= 2
    t_i, s_i = _iota2((L, L), 0), _iota2((L, L), 1)
    r_w, c_w = _iota2((2 * L, 4 * L), 0), _iota2((2 * L, 4 * L), 1) & (2 * L - 1)
    consts = ((t_i >= s_i).astype(BF16), r_w > c_w, r_w >= c_w, _iota2((1, 128), 1) < 64, level_masks,
              (r_i == c_i).astype(F32))
    prep_refs = (r_s, k_s, v_s, kk_s, a_s, ld_s, rhat_s, yv_s, mt_s, gt_s, dec_s)
    lax.fori_loop(0, rows // (L * RWKV_PREP_CHUNKS), lambda c, carry: _rwkv_prepare_chunk(c, prep_refs, consts), 0)
    scan_refs = (rhat_s, yv_s, mt_s, gt_s, dec_s, y_s, state_ref)
    lax.fori_loop(0, rows // L, lambda c, carry: _rwkv_scan_chunk(c, scan_refs), 0, unroll=4)

    wout = wout_ref[...]
    inv_n = 1.0 / RWKV_HEAD_DIM
    for rs, (gate, y_lru) in zip(subs, fronts):
        y = y_s[rs, :]
        mean = _head_sum(y, ones_bd) * inv_n
        yc = y - mean
        var = _head_sum(yc * yc, ones_bd) * inv_n
        y = yc * lax.rsqrt(var + RWKV_GN_EPS) * lng_ref[...] + lnb_ref[...]
        y = y + _head_sum(r_s[rs, :] * k_s[rs, :] * rk_ref[...], ones_bd) * v_s[rs, :]
        o_ref[rs, :] = (h_ref[rs, :] + _mm(y * gate, wout[:RWKV_WIDTH, :]) + _mm(y_lru, wout[RWKV_WIDTH:, :]))


def _pad_rows(w, n):
    return jnp.pad(w, ((0, n - w.shape[0]), (0, 0)))


def _pad_cols(w, n):
    return jnp.pad(w, ((0, 0), (0, n - w.shape[-1])))


def _odd_layer(h, ng, w_in, w_out, mu, w0, w_up, a0, a_up, g_up, k_k, k_a, r_k, ln_g, ln_b, lconv_w, lconv_b,
               wa, ba, wx, bx, lam, rows):
    W, P = RWKV_WIDTH, RWKV_LORA_PAD
    o = 3 * W
    dr, ir = w_up.shape[0], a_up.shape[0]

    def pad_sections(m):
        return jnp.concatenate([m[..., :o], _pad_cols(m[..., o:o + dr], P), _pad_cols(m[..., o + dr:o + dr + ir], P),
                                m[..., o + dr + ir:]], axis=-1)

    w_in_p = pad_sections(w_in).astype(BF16)
    mu_p = pad_sections(mu.reshape(1, -1))
    ones_bd = _block_diag(jnp.ones((W // RWKV_HEAD_DIM, RWKV_HEAD_DIM, RWKV_HEAD_DIM), BF16))
    wax = jnp.concatenate([_block_diag(wa), _block_diag(wx)], axis=1).astype(BF16)
    consts = [_row(ng), w_in_p, w_out.astype(BF16), mu_p.astype(F32), _row(w0), _pad_rows(w_up, P).astype(BF16),
              _row(a0), _pad_rows(a_up, P).astype(BF16), g_up.astype(BF16), _row(k_k), _row(k_a), _row(r_k),
              _row(ln_g), _row(ln_b), lconv_w.astype(F32), _row(lconv_b), wax,
              _row(jnp.concatenate([ba, bx])), _row(lam), ones_bd]
    scratch = [pltpu.VMEM((CONV_HALO, RWKV_SHIFT_WIDTH), F32), pltpu.VMEM((CONV_HALO, LRU_WIDTH), F32),
               pltpu.VMEM((CONV_HALO, LRU_WIDTH), F32), pltpu.VMEM((RWKV_PAIRS, 128, 128), F32)]
    scratch += [pltpu.VMEM((rows, W), F32) for _ in range(7)]
    n_pc = (rows // RWKV_CHUNK) * RWKV_PAIRS
    scratch += [pltpu.VMEM((n_pc, 128, 128), BF16), pltpu.VMEM((n_pc, RWKV_CHUNK, 128), F32),
                pltpu.VMEM((n_pc, 128, 128), BF16), pltpu.VMEM((n_pc, 128, 128), F32),
                pltpu.VMEM((rows // RWKV_CHUNK * 8, W), F32)]
    return _seq_call(_odd_body, "odd_mixer", h, consts, scratch, rows)


def kernel(x, mix_norm_g, ffn_norm_g, ffn_w_gate, ffn_w_up, ffn_w_down, final_norm_g, ev_w_in, ev_w_out, pool_w,
           pool_scale, ssd_conv_w, ssd_conv_b, ssd_dt_bias, ssd_a_log, ssd_d, ssd_norm_g, od_w_in, od_w_out,
           rwkv_mu, rwkv_w0, rwkv_w_up, rwkv_a0, rwkv_a_up, rwkv_g_up, rwkv_k_k, rwkv_k_a, rwkv_r_k, rwkv_ln_g,
           rwkv_ln_b, lru_conv_w, lru_conv_b, lru_wa, lru_ba, lru_wx, lru_bx, lru_lambda):
    bsz, seq, d = x.shape
    assert d == D_MODEL and x.dtype == F32
    rows = min(SEQ_TILE, seq)
    assert seq % rows == 0 and rows % SSD_CHUNK == 0
    depth = mix_norm_g.shape[0]
    h = x
    for layer in range(depth):
        i = layer // 2
        if layer % 2 == 0:
            h = _even_layer(h, mix_norm_g[layer], ev_w_in[i], ev_w_out[i], pool_w[i], pool_scale[i], ssd_conv_w[i],
                            ssd_conv_b[i], ssd_dt_bias[i], ssd_a_log[i], ssd_d[i], ssd_norm_g[i], rows)
        else:
            h = _odd_layer(h, mix_norm_g[layer], od_w_in[i], od_w_out[i], rwkv_mu[i], rwkv_w0[i], rwkv_w_up[i],
                           rwkv_a0[i], rwkv_a_up[i], rwkv_g_up[i], rwkv_k_k[i], rwkv_k_a[i],
                           rwkv_r_k[i].reshape(-1), rwkv_ln_g[i], rwkv_ln_b[i], lru_conv_w[i], lru_conv_b[i],
                           lru_wa[i], lru_ba[i], lru_wx[i], lru_bx[i], lru_lambda[i], rows)
        final = layer == depth - 1
        h = _ffn(h.reshape(bsz * seq, d), _row(ffn_norm_g[layer]), ffn_w_gate[layer].astype(BF16),
                 ffn_w_up[layer].astype(BF16), ffn_w_down[layer].astype(BF16), _row(final_norm_g),
                 final).reshape(bsz, seq, d)
    return h
```

```python
import functools
import math

import jax
import jax.numpy as jnp
from jax import lax
from jax.experimental import pallas as pl
from jax.experimental.pallas import tpu as pltpu

F32 = jnp.float32
BF16 = jnp.bfloat16

D_MODEL = 1024
D_FF = 2816
RMS_EPS = 1e-6

POOL_WIDTH = 256
POOL_WINDOWS = (2, 4, 8, 16)
POOL_GROUP_DIM = 64
POOL_HALO = 16

SSD_WIDTH = 768
SSD_HEADS = 12
SSD_HEAD_DIM = 64
SSD_GROUPS = 2
SSD_GROUP_WIDTH = SSD_WIDTH // SSD_GROUPS
SSD_STATE = 128
SSD_CHUNK = 128
SSD_CONV_DIM = 1280
CONV_K = 4
CONV_HALO = 8

RWKV_WIDTH = 512
RWKV_HEAD_DIM = 64
RWKV_PAIRS = RWKV_WIDTH // 128
RWKV_CHUNK = 64
RWKV_PREP_CHUNKS = 4
RWKV_LORA_PAD = 128
RWKV_SHIFT_WIDTH = 3 * RWKV_WIDTH + 3 * RWKV_LORA_PAD
RWKV_GN_EPS = 64e-5
RWKV_DECAY_OFFSET = 0.5
LRU_WIDTH = 512
LRU_C = 8.0

SSD_DT_PAD = 128
ODD_IN_PAD = RWKV_SHIFT_WIDTH + 2 * LRU_WIDTH

SEQ_TILE = 512
SUB_ROWS = 256
FFN_ROWS = 512
VMEM_LIMIT = 56 * 1024 * 1024


def _mm(a, b):
    return jnp.dot(a.astype(BF16), b.astype(BF16), preferred_element_type=F32)


def _mm_nt(a, b):
    return lax.dot_general(a.astype(BF16), b.astype(BF16), (((1,), (1,)), ((), ())),
                           preferred_element_type=F32)


def _mm_tn(a, b):
    return lax.dot_general(a.astype(BF16), b.astype(BF16), (((0,), (0,)), ((), ())),
                           preferred_element_type=F32)


def _split_bf16(x, parts):
    out = []
    for _ in range(parts - 1):
        hi = x.astype(BF16)
        out.append(hi)
        x = x - hi.astype(F32)
    out.append(x.astype(BF16))
    return out


def _cumsum_rows(tri, x, parts):
    return sum(jnp.dot(tri, part, preferred_element_type=F32) for part in _split_bf16(x, parts))


def _head_sum(x, ones_bd):
    return jnp.dot(x.astype(BF16), ones_bd, preferred_element_type=F32)


def _rmsnorm(x, g):
    ms = jnp.mean(x * x, axis=-1, keepdims=True)
    return x * lax.rsqrt(ms + RMS_EPS) * g


def _silu(x):
    return x * jax.nn.sigmoid(x)


def _iota2(shape, axis):
    return lax.broadcasted_iota(jnp.int32, shape, axis)


def _causal_conv(x, prev_ref, w, b, rows):
    xe = jnp.concatenate([prev_ref[...], x], axis=0)
    prev_ref[...] = x[rows - CONV_HALO:, :]
    x1 = pltpu.roll(xe, 1, 0)
    acc = xe * w[3:4, :] + x1 * w[2:3, :] + pltpu.roll(xe * w[1:2, :] + x1 * w[0:1, :], 2, 0)
    return acc[CONV_HALO:, :] + b


def _ffn_body(h_ref, g_ref, wg_ref, wu_ref, wd_ref, fg_ref, o_ref, *, final):
    h = h_ref[...]
    hn = _rmsnorm(h, g_ref[...]).astype(BF16)
    gate = jnp.dot(hn, wg_ref[...], preferred_element_type=F32)
    up = jnp.dot(hn, wu_ref[...], preferred_element_type=F32)
    act = (_silu(gate) * up).astype(BF16)
    out = h + jnp.dot(act, wd_ref[...], preferred_element_type=F32)
    if final:
        out = _rmsnorm(out, fg_ref[...])
    o_ref[...] = out


def _resident(shape):
    nd = len(shape)
    return pl.BlockSpec(shape, lambda *_: (0,) * nd, pipeline_mode=pl.Buffered(1))


def _layer_of_stack(shape, layer):
    return pl.BlockSpec((pl.Squeezed(),) + tuple(shape), lambda *_: (layer, 0, 0), pipeline_mode=pl.Buffered(1))


def _ffn(h2d, g, wg, wu, wd, fg, layer, final):
    t = h2d.shape[0]
    rows = min(FFN_ROWS, t)
    row_spec = pl.BlockSpec((rows, D_MODEL), lambda i: (i, 0))
    return pl.pallas_call(
        functools.partial(_ffn_body, final=final),
        grid=(t // rows,),
        in_specs=[row_spec, _resident((1, D_MODEL)), _layer_of_stack((D_MODEL, D_FF), layer),
                  _layer_of_stack((D_MODEL, D_FF), layer), _layer_of_stack((D_FF, D_MODEL), layer),
                  _resident((1, D_MODEL))],
        out_specs=row_spec,
        out_shape=jax.ShapeDtypeStruct((t, D_MODEL), F32),
        compiler_params=pltpu.CompilerParams(dimension_semantics=("arbitrary",), vmem_limit_bytes=VMEM_LIMIT),
        name="ffn_final" if final else "ffn",
    )(h2d, g, wg, wu, wd, fg)


def _pool_mixer(u, prev_ref, poolw_bd, pscale, first_pos, rows):
    ue = jnp.concatenate([prev_ref[...], u], axis=0)
    prev_ref[...] = u[rows - POOL_HALO:, :]
    sums = []
    acc = ue
    shift = 1
    for _ in POOL_WINDOWS:
        acc = acc + pltpu.roll(acc, shift, 0)
        sums.append(acc[POOL_HALO:, :])
        shift *= 2
    pos = (first_pos + 1 + _iota2((rows, 1), 0)).astype(F32)
    lane = _iota2((1, POOL_WIDTH), 1)
    pooled = sums[-1] * (1.0 / jnp.minimum(pos, float(POOL_WINDOWS[-1])))
    for gi in range(len(POOL_WINDOWS) - 2, -1, -1):
        cand = sums[gi] * (1.0 / jnp.minimum(pos, float(POOL_WINDOWS[gi])))
        pooled = jnp.where(lane < (gi + 1) * POOL_GROUP_DIM, cand, pooled)
    return _mm(pooled - u, poolw_bd) * pscale


def _ssd_chunk(c, refs, consts):
    xc_s, da_s, b_s, c_s, y_s, state_ref = refs
    tri, causal, lane_lo = consts
    L = SSD_CHUNK
    rows = pl.ds(c * L, L)
    cum = _cumsum_rows(tri, da_s[rows, :], 2)
    xc = xc_s[rows, :]
    bmat = b_s[rows, :]
    cmat = c_s[rows, :]
    last = cum[L - 1:L, :]
    ecum = jnp.exp(cum)
    to_end = jnp.exp(last - cum)
    chunk_decay = jnp.exp(last)
    for g in range(SSD_GROUPS):
        gs = slice(g * SSD_GROUP_WIDTH, (g + 1) * SSD_GROUP_WIDTH)
        ns = slice(g * SSD_STATE, (g + 1) * SSD_STATE)
        bg, cg = bmat[:, ns], cmat[:, ns]
        cb = _mm_nt(cg, bg)
        hprev = state_ref[g]
        y_g = _mm(cg, hprev) * ecum[:, gs]
        ypairs = []
        for k in range(SSD_GROUP_WIDTH // 128):
            ps = slice(g * SSD_GROUP_WIDTH + k * 128, g * SSD_GROUP_WIDTH + (k + 1) * 128)
            slab = cum[:, ps]
            swapped = pltpu.roll(slab, 64, 1)
            xpair = xc[:, ps]
            outs = []
            for col in (jnp.where(lane_lo, slab, swapped), jnp.where(lane_lo, swapped, slab)):
                seg = col - col.T
                decay = jnp.where(causal, jnp.exp(jnp.minimum(seg, 0.0)), 0.0)
                outs.append(_mm(cb * decay, xpair))
            ypairs.append(jnp.where(lane_lo, outs[0], outs[1]))
        y_s[rows, gs] = y_g + jnp.concatenate(ypairs, axis=1)
        state_ref[g] = hprev * chunk_decay[:, gs] + _mm_tn(bg, xc[:, gs] * to_end[:, gs])
    return 0


def _even_body(h_ref, ng_ref, win_ref, wout_ref, poolw_ref, pscale_ref, convw_ref, convb_ref, dtb_ref, alog_ref,
               dskip_ref, ssdg_ref, expand_ref, o_ref, upool_prev, xbc_prev, state_ref, xc_s, da_s, b_s, c_s, y_s, *,
               rows):
    s = pl.program_id(1)

    @pl.when(s == 0)
    def _():
        upool_prev[...] = jnp.zeros_like(upool_prev)
        xbc_prev[...] = jnp.zeros_like(xbc_prev)
        state_ref[...] = jnp.zeros_like(state_ref)

    o1, o2, o3 = POOL_WIDTH, POOL_WIDTH + SSD_WIDTH, POOL_WIDTH + SSD_WIDTH + SSD_CONV_DIM
    sub = min(SUB_ROWS, rows)
    subs = [slice(j * sub, (j + 1) * sub) for j in range(rows // sub)]

    L = SSD_CHUNK
    r_i, c_i = _iota2((L, L), 0), _iota2((L, L), 1)
    consts = ((r_i >= c_i).astype(BF16), r_i >= c_i, _iota2((1, 128), 1) < 64)
    refs = (xc_s, da_s, b_s, c_s, y_s, state_ref)
    wout = wout_ref[...]

    def front(j, rs):
        p = _mm(_rmsnorm(h_ref[rs, :], ng_ref[...]), win_ref[...])
        y_pool = _pool_mixer(p[:, :o1], upool_prev, poolw_ref[...], pscale_ref[...], s * rows + j * sub, sub)
        xbc = _silu(_causal_conv(p[:, o2:o3], xbc_prev, convw_ref[...], convb_ref[...], sub))
        xh = xbc[:, :SSD_WIDTH]
        dt_heads = jax.nn.softplus(p[:, o3:] + dtb_ref[...])
        dt = sum(jnp.dot(part, expand_ref[...], preferred_element_type=F32) for part in _split_bf16(dt_heads, 3))
        xc_s[rs, :] = xh * dt
        da_s[rs, :] = dt * (-jnp.exp(alog_ref[...]))
        b_s[rs, :] = xbc[:, SSD_WIDTH:SSD_WIDTH + SSD_GROUPS * SSD_STATE]
        c_s[rs, :] = xbc[:, SSD_WIDTH + SSD_GROUPS * SSD_STATE:]
        return y_pool, xh, p[:, o1:o2]

    def scan(j):
        for c in range(j * (sub // L), (j + 1) * (sub // L)):
            _ssd_chunk(c, refs, consts)

    def tail(rs, y_pool, xh, z):
        y = (y_s[rs, :] + dskip_ref[...] * xh) * _silu(z)
        normed = []
        for g in range(SSD_GROUPS):
            yy = y[:, g * SSD_GROUP_WIDTH:(g + 1) * SSD_GROUP_WIDTH]
            normed.append(yy * lax.rsqrt(jnp.mean(yy * yy, axis=-1, keepdims=True) + RMS_EPS))
        y_ssd = jnp.concatenate(normed, axis=1) * ssdg_ref[...]
        o_ref[rs, :] = h_ref[rs, :] + _mm(y_pool, wout[:POOL_WIDTH, :]) + _mm(y_ssd, wout[POOL_WIDTH:, :])

    fronts = [front(j, rs) for j, rs in enumerate(subs)]
    for j in range(len(subs)):
        scan(j)
    for rs, parts in zip(subs, fronts):
        tail(rs, *parts)


def _seq_call(body, name, h, consts, scratch, rows):
    bsz, seq, _ = h.shape
    tile = pl.BlockSpec((pl.Squeezed(), rows, D_MODEL), lambda b, s: (b, s, 0))
    return pl.pallas_call(
        functools.partial(body, rows=rows),
        grid=(bsz, seq // rows),
        in_specs=[tile] + [_resident(c.shape) for c in consts],
        out_specs=tile,
        out_shape=jax.ShapeDtypeStruct(h.shape, F32),
        scratch_shapes=scratch,
        compiler_params=pltpu.CompilerParams(dimension_semantics=("arbitrary", "arbitrary"),
                                             vmem_limit_bytes=VMEM_LIMIT),
        name=name,
    )(h, *consts)


def _block_diag(w):
    g, c, _ = w.shape
    eye = jnp.eye(g, dtype=w.dtype)
    return (eye[:, None, :, None] * w[:, :, None, :]).reshape(g * c, g * c)


def _row(v):
    return v.reshape(1, -1).astype(F32)


def _even_layer(h, ng, w_in, w_out, pool_w, pool_scale, conv_w, conv_b, dt_bias, a_log, d_skip, norm_g, rows):
    o3 = POOL_WIDTH + SSD_WIDTH + SSD_CONV_DIM
    rep = lambda v: jnp.repeat(v, SSD_HEAD_DIM, axis=-1)
    w_in_p = jnp.concatenate([w_in[:, :o3], _pad_cols(w_in[:, o3:], SSD_DT_PAD)], axis=1).astype(BF16)
    head_of_channel = jnp.arange(SSD_WIDTH) // SSD_HEAD_DIM
    expand = (jnp.arange(SSD_DT_PAD)[:, None] == head_of_channel[None, :]).astype(BF16)
    consts = [_row(ng), w_in_p, w_out.astype(BF16), _block_diag(pool_w).astype(BF16), _row(pool_scale),
              conv_w.astype(F32), _row(conv_b), _pad_cols(_row(dt_bias), SSD_DT_PAD), _row(rep(a_log)),
              _row(rep(d_skip)), _row(norm_g), expand]
    scratch = [pltpu.VMEM((POOL_HALO, POOL_WIDTH), F32), pltpu.VMEM((CONV_HALO, SSD_CONV_DIM), F32),
               pltpu.VMEM((SSD_GROUPS, SSD_STATE, SSD_GROUP_WIDTH), F32),
               pltpu.VMEM((rows, SSD_WIDTH), F32), pltpu.VMEM((rows, SSD_WIDTH), F32),
               pltpu.VMEM((rows, SSD_GROUPS * SSD_STATE), F32), pltpu.VMEM((rows, SSD_GROUPS * SSD_STATE), F32),
               pltpu.VMEM((rows, SSD_WIDTH), F32)]
    return _seq_call(_even_body, "even_mixer", h, consts, scratch, rows)


def _stack_heads(x, lane_lo):
    return jnp.concatenate([jnp.where(lane_lo, x, 0.0), jnp.where(lane_lo, 0.0, x)], axis=0)


def _chunk_dot(a, b, ca, cb):
    return lax.dot_general(a.astype(BF16), b.astype(BF16), (((ca,), (cb,)), ((), ())), preferred_element_type=F32)


def _rwkv_prepare_chunk(c, refs, consts):
    r_s, k_s, v_s, kk_s, a_s, ld_s, rhat_s, yv_s, mt_s, gt_s, dec_s = refs
    tri, strict, incl, lane_lo, level_masks, eye = consts
    L = RWKV_CHUNK
    ops = []
    for j in range(RWKV_PREP_CHUNKS):
        cj = c * RWKV_PREP_CHUNKS + j
        rows = pl.ds(pl.multiple_of(cj * L, L), L)
        ld = ld_s[rows, :]
        cum = _cumsum_rows(tri, ld, 3)
        last = cum[L - 1:L, :]
        e_neg, e_end = jnp.exp(-cum), jnp.exp(last - cum)
        kk = kk_s[rows, :]
        kka = kk * a_s[rows, :]
        k = k_s[rows, :]
        ops.append(dict(a=-kk * jnp.exp(cum - ld), r=r_s[rows, :] * jnp.exp(cum), b=kka * e_neg, k=k * e_neg,
                        be=kka * e_end, ke=k * e_end, v=v_s[rows, :]))
        dec_s[pl.ds(pl.multiple_of(cj * 8, 8), 8), :] = jnp.broadcast_to(jnp.exp(last), (8, RWKV_WIDTH))

    items = [(j, q) for j in range(RWKV_PREP_CHUNKS) for q in range(RWKV_PAIRS)]
    P = range(len(items))
    st = lambda name, i: _stack_heads(ops[items[i][0]][name][:, items[i][1] * 128:(items[i][1] + 1) * 128], lane_lo)
    a2 = [st("a", i) for i in P]
    r2 = [st("r", i) for i in P]
    bk2 = [jnp.concatenate([st("b", i), st("k", i)], axis=0) for i in P]
    v2 = [st("v", i) for i in P]
    n_a = [jnp.where(strict, _chunk_dot(a2[i], bk2[i], 1, 1), 0.0) for i in P]
    n_r = [jnp.where(incl, _chunk_dot(r2[i], bk2[i], 1, 1), 0.0) for i in P]
    n_ab = [n_a[i][:, :2 * L] for i in P]

    x = [eye + jnp.where(level_masks[0], n_ab[i], 0.0) for i in P]
    for m in level_masks[1:]:
        t = [_chunk_dot(jnp.where(m, n_ab[i], 0.0), x[i], 1, 0) for i in P]
        x = [x[i] + _chunk_dot(x[i], t[i], 1, 0) for i in P]

    akv = [_chunk_dot(n_a[i][:, 2 * L:], v2[i], 1, 0) for i in P]
    au = [_chunk_dot(x[i], jnp.concatenate([a2[i], akv[i]], axis=1), 1, 0) for i in P]
    ru = [_chunk_dot(n_r[i][:, :2 * L], au[i], 1, 0) for i in P]
    rkv = [_chunk_dot(n_r[i][:, 2 * L:], v2[i], 1, 0) for i in P]
    mg = [_chunk_dot(au[i], st("be", i), 0, 0) for i in P]
    vk = [_chunk_dot(v2[i], st("ke", i), 0, 0) for i in P]
    for i in P:
        idx = (c * RWKV_PREP_CHUNKS + items[i][0]) * RWKV_PAIRS + items[i][1]
        rhat_s[idx] = (r2[i] + ru[i][:, :2 * L]).astype(BF16)
        yv = ru[i][:, 2 * L:] + rkv[i]
        yv_s[idx] = yv[:L, :] + yv[L:, :]
        mt_s[idx] = mg[i][:2 * L, :].astype(BF16)
        gt_s[idx] = mg[i][2 * L:, :] + vk[i]
    return 0


def _rwkv_scan_chunk(c, refs):
    rhat_s, yv_s, mt_s, gt_s, dec_s, y_s, state_ref = refs
    L = RWKV_CHUNK
    rows = pl.ds(pl.multiple_of(c * L, L), L)
    decay = dec_s[pl.ds(pl.multiple_of(c * 8, 8), 8), :][:1, :]
    for q in range(RWKV_PAIRS):
        idx = c * RWKV_PAIRS + q
        sl = slice(q * 128, (q + 1) * 128)
        state = state_ref[q]
        y2 = _chunk_dot(rhat_s[idx], state, 1, 1)
        y_s[rows, sl] = y2[:L, :] + y2[L:, :] + yv_s[idx]
        state_ref[q] = state * decay[:, sl] + _chunk_dot(state, mt_s[idx], 1, 0) + gt_s[idx]
    return 0


def _lru_scan(a, b, carry_ref, rows):
    groups, width = rows // 8, a.shape[1]
    a3, b3 = a.reshape(groups, 8, width), b.reshape(groups, 8, width)
    sub = _iota2((1, 8, 1), 1)
    d = 1
    while d < 8:
        valid = sub >= d
        b3 = jnp.where(valid, a3 * pltpu.roll(b3, d, 1) + b3, b3)
        a3 = jnp.where(valid, a3 * pltpu.roll(a3, d, 1), a3)
        d *= 2
    h_in = carry_ref[CONV_HALO - 1:CONV_HALO, :]
    blocks = []
    for g in range(groups):
        blocks.append(b3[g] + a3[g] * h_in)
        h_in = blocks[-1][7:8, :]
    carry_ref[...] = blocks[-1]
    return jnp.concatenate(blocks, axis=0)


def _odd_body(h_ref, ng_ref, win_ref, wout_ref, mu_ref, w0_ref, wup_ref, a0_ref, aup_ref, gup_ref, kk_ref, ka_ref,
              rk_ref, lng_ref, lnb_ref, lcw_ref, lcb_ref, wax_ref, bax_ref, lam_ref, ones_ref, o_ref,
              shift_prev, lru_prev, lru_carry, state_ref, r_s, k_s, v_s, kk_s, a_s, ld_s, y_s,
              rhat_s, yv_s, mt_s, gt_s, dec_s, *, rows):
    s = pl.program_id(1)

    @pl.when(s == 0)
    def _():
        shift_prev[...] = jnp.zeros_like(shift_prev)
        lru_prev[...] = jnp.zeros_like(lru_prev)
        lru_carry[...] = jnp.zeros_like(lru_carry)
        state_ref[...] = jnp.zeros_like(state_ref)

    W = RWKV_WIDTH
    ones_bd = ones_ref[...]
    sub = rows
    subs = [slice(j * sub, (j + 1) * sub) for j in range(rows // sub)]

    def front(rs):
        p = _mm(_rmsnorm(h_ref[rs, :], ng_ref[...]), win_ref[...])
        pr = p[:, :RWKV_SHIFT_WIDTH]
        prev = pltpu.roll(jnp.concatenate([shift_prev[...], pr], axis=0), 1, 0)[CONV_HALO:, :]
        shift_prev[...] = pr[sub - CONV_HALO:, :]
        ps = pr + mu_ref[...] * (prev - pr)
        r, k, v = ps[:, :W], ps[:, W:2 * W], ps[:, 2 * W:3 * W]
        o = 3 * W
        wd, ad, gd = (ps[:, o + i * RWKV_LORA_PAD:o + (i + 1) * RWKV_LORA_PAD] for i in range(3))
        log_decay = -math.exp(-RWKV_DECAY_OFFSET) * jax.nn.sigmoid(w0_ref[...] + _mm(jnp.tanh(wd), wup_ref[...]))
        a = jax.nn.sigmoid(a0_ref[...] + _mm(ad, aup_ref[...]))
        gate = _mm(jax.nn.sigmoid(gd), gup_ref[...])
        kk = k * kk_ref[...]
        kk = kk * lax.rsqrt(_head_sum(kk * kk, ones_bd) + 1e-12)
        k = k * (1.0 + (a - 1.0) * ka_ref[...])
        r_s[rs, :], k_s[rs, :], v_s[rs, :], kk_s[rs, :], a_s[rs, :] = r, k, v, kk, a
        ld_s[rs, :] = log_decay

        gate_l = p[:, RWKV_SHIFT_WIDTH:RWKV_SHIFT_WIDTH + LRU_WIDTH]
        xb = _causal_conv(p[:, RWKV_SHIFT_WIDTH + LRU_WIDTH:], lru_prev, lcw_ref[...], lcb_ref[...], sub)
        gates = jax.nn.sigmoid(_mm(xb, wax_ref[...]) + bax_ref[...])
        log_a = -LRU_C * gates[:, :LRU_WIDTH] * jax.nn.softplus(-lam_ref[...])
        a_l = jnp.exp(log_a)
        b_l = jnp.sqrt(-jnp.tanh(log_a) * (a_l * a_l + 1.0)) * gates[:, LRU_WIDTH:] * xb
        y_lru = _lru_scan(a_l, b_l, lru_carry, sub) * jax.nn.gelu(gate_l)
        return gate, y_lru

    fronts = [front(rs) for rs in subs]

    L = RWKV_CHUNK
    r_i, c_i = _iota2((2 * L, 2 * L), 0), _iota2((2 * L, 2 * L), 1)
    level_masks = []
    b = 1
    while b < L:
        same_block = (r_i & -(2 * b)) == (c_i & -(2 * b))
        level_masks.append(same_block & ((r_i & b) != 0) & ((c_i & b) == 0))
        b *= 2
    t_i, s_i = _iota2((L, L), 0), _iota2((L, L), 1)
    r_w, c_w = _iota2((2 * L, 4 * L), 0), _iota2((2 * L, 4 * L), 1) & (2 * L - 1)
    consts = ((t_i >= s_i).astype(BF16), r_w > c_w, r_w >= c_w, _iota2((1, 128), 1) < 64, level_masks,
              (r_i == c_i).astype(F32))
    prep_refs = (r_s, k_s, v_s, kk_s, a_s, ld_s, rhat_s, yv_s, mt_s, gt_s, dec_s)
    lax.fori_loop(0, rows // (L * RWKV_PREP_CHUNKS), lambda c, carry: _rwkv_prepare_chunk(c, prep_refs, consts), 0)
    scan_refs = (rhat_s, yv_s, mt_s, gt_s, dec_s, y_s, state_ref)
    lax.fori_loop(0, rows // L, lambda c, carry: _rwkv_scan_chunk(c, scan_refs), 0, unroll=4)

    wout = wout_ref[...]
    inv_n = 1.0 / RWKV_HEAD_DIM
    for rs, (gate, y_lru) in zip(subs, fronts):
        y = y_s[rs, :]
        mean = _head_sum(y, ones_bd) * inv_n
        yc = y - mean
        var = _head_sum(yc * yc, ones_bd) * inv_n
        y = yc * lax.rsqrt(var + RWKV_GN_EPS) * lng_ref[...] + lnb_ref[...]
        y = y + _head_sum(r_s[rs, :] * k_s[rs, :] * rk_ref[...], ones_bd) * v_s[rs, :]
        o_ref[rs, :] = (h_ref[rs, :] + _mm(y * gate, wout[:RWKV_WIDTH, :]) + _mm(y_lru, wout[RWKV_WIDTH:, :]))


def _pad_rows(w, n):
    return jnp.pad(w, ((0, n - w.shape[0]), (0, 0)))


def _pad_cols(w, n):
    return jnp.pad(w, ((0, 0), (0, n - w.shape[-1])))


def _odd_layer(h, ng, w_in, w_out, mu, w0, w_up, a0, a_up, g_up, k_k, k_a, r_k, ln_g, ln_b, lconv_w, lconv_b,
               wa, ba, wx, bx, lam, rows):
    W, P = RWKV_WIDTH, RWKV_LORA_PAD
    o = 3 * W
    dr, ir = w_up.shape[0], a_up.shape[0]

    def pad_sections(m):
        return jnp.concatenate([m[..., :o], _pad_cols(m[..., o:o + dr], P), _pad_cols(m[..., o + dr:o + dr + ir], P),
                                m[..., o + dr + ir:]], axis=-1)

    w_in_p = pad_sections(w_in).astype(BF16)
    mu_p = pad_sections(mu.reshape(1, -1))
    ones_bd = _block_diag(jnp.ones((W // RWKV_HEAD_DIM, RWKV_HEAD_DIM, RWKV_HEAD_DIM), BF16))
    wax = jnp.concatenate([_block_diag(wa), _block_diag(wx)], axis=1).astype(BF16)
    consts = [_row(ng), w_in_p, w_out.astype(BF16), mu_p.astype(F32), _row(w0), _pad_rows(w_up, P).astype(BF16),
              _row(a0), _pad_rows(a_up, P).astype(BF16), g_up.astype(BF16), _row(k_k), _row(k_a), _row(r_k),
              _row(ln_g), _row(ln_b), lconv_w.astype(F32), _row(lconv_b), wax,
              _row(jnp.concatenate([ba, bx])), _row(lam), ones_bd]
    scratch = [pltpu.VMEM((CONV_HALO, RWKV_SHIFT_WIDTH), F32), pltpu.VMEM((CONV_HALO, LRU_WIDTH), F32),
               pltpu.VMEM((CONV_HALO, LRU_WIDTH), F32), pltpu.VMEM((RWKV_PAIRS, 128, 128), F32)]
    scratch += [pltpu.VMEM((rows, W), F32) for _ in range(7)]
    n_pc = (rows // RWKV_CHUNK) * RWKV_PAIRS
    scratch += [pltpu.VMEM((n_pc, 128, 128), BF16), pltpu.VMEM((n_pc, RWKV_CHUNK, 128), F32),
                pltpu.VMEM((n_pc, 128, 128), BF16), pltpu.VMEM((n_pc, 128, 128), F32),
                pltpu.VMEM((rows // RWKV_CHUNK * 8, W), F32)]
    return _seq_call(_odd_body, "odd_mixer", h, consts, scratch, rows)


def kernel(x, mix_norm_g, ffn_norm_g, ffn_w_gate, ffn_w_up, ffn_w_down, final_norm_g, ev_w_in, ev_w_out, pool_w,
           pool_scale, ssd_conv_w, ssd_conv_b, ssd_dt_bias, ssd_a_log, ssd_d, ssd_norm_g, od_w_in, od_w_out,
           rwkv_mu, rwkv_w0, rwkv_w_up, rwkv_a0, rwkv_a_up, rwkv_g_up, rwkv_k_k, rwkv_k_a, rwkv_r_k, rwkv_ln_g,
           rwkv_ln_b, lru_conv_w, lru_conv_b, lru_wa, lru_ba, lru_wx, lru_bx, lru_lambda):
    bsz, seq, d = x.shape
    assert d == D_MODEL and x.dtype == F32
    rows = min(SEQ_TILE, seq)
    assert seq % rows == 0 and rows % SSD_CHUNK == 0
    depth = mix_norm_g.shape[0]
    w_gate, w_up, w_down = ffn_w_gate.astype(BF16), ffn_w_up.astype(BF16), ffn_w_down.astype(BF16)
    h = x
    for layer in range(depth):
        i = layer // 2
        if layer % 2 == 0:
            h = _even_layer(h, mix_norm_g[layer], ev_w_in[i], ev_w_out[i], pool_w[i], pool_scale[i], ssd_conv_w[i],
                            ssd_conv_b[i], ssd_dt_bias[i], ssd_a_log[i], ssd_d[i], ssd_norm_g[i], rows)
        else:
            h = _odd_layer(h, mix_norm_g[layer], od_w_in[i], od_w_out[i], rwkv_mu[i], rwkv_w0[i], rwkv_w_up[i],
                           rwkv_a0[i], rwkv_a_up[i], rwkv_g_up[i], rwkv_k_k[i], rwkv_k_a[i],
                           rwkv_r_k[i].reshape(-1), rwkv_ln_g[i], rwkv_ln_b[i], lru_conv_w[i], lru_conv_b[i],
                           lru_wa[i], lru_ba[i], lru_wx[i], lru_bx[i], lru_lambda[i], rows)
        final = layer == depth - 1
        h = _ffn(h.reshape(bsz * seq, d), _row(ffn_norm_g[layer]), w_gate, w_up, w_down, _row(final_norm_g),
                 layer, final).reshape(bsz, seq, d)
    return h
```

```python
import functools
import math

import jax
import jax.numpy as jnp
from jax import lax
from jax.experimental import pallas as pl
from jax.experimental.pallas import tpu as pltpu

F32 = jnp.float32
BF16 = jnp.bfloat16

D_MODEL = 1024
D_FF = 2816
RMS_EPS = 1e-6

POOL_WIDTH = 256
POOL_WINDOWS = (2, 4, 8, 16)
POOL_GROUP_DIM = 64
POOL_HALO = 16

SSD_WIDTH = 768
SSD_HEADS = 12
SSD_HEAD_DIM = 64
SSD_GROUPS = 2
SSD_GROUP_WIDTH = SSD_WIDTH // SSD_GROUPS
SSD_STATE = 128
SSD_CHUNK = 128
SSD_CONV_DIM = 1280
CONV_K = 4
CONV_HALO = 8

RWKV_WIDTH = 512
RWKV_HEAD_DIM = 64
RWKV_PAIRS = RWKV_WIDTH // 128
RWKV_CHUNK = 64
RWKV_PREP_CHUNKS = 4
RWKV_LORA_PAD = 128
RWKV_SHIFT_WIDTH = 3 * RWKV_WIDTH + 3 * RWKV_LORA_PAD
RWKV_GN_EPS = 64e-5
RWKV_DECAY_OFFSET = 0.5
LRU_WIDTH = 512
LRU_C = 8.0

SSD_DT_PAD = 128
ODD_IN_PAD = RWKV_SHIFT_WIDTH + 2 * LRU_WIDTH

SEQ_TILE = 512
SUB_ROWS = 256
FFN_ROWS = 512
VMEM_LIMIT = 56 * 1024 * 1024


def _mm(a, b):
    return jnp.dot(a.astype(BF16), b.astype(BF16), preferred_element_type=F32)


def _mm_nt(a, b):
    return lax.dot_general(a.astype(BF16), b.astype(BF16), (((1,), (1,)), ((), ())),
                           preferred_element_type=F32)


def _mm_tn(a, b):
    return lax.dot_general(a.astype(BF16), b.astype(BF16), (((0,), (0,)), ((), ())),
                           preferred_element_type=F32)


def _split_bf16(x, parts):
    out = []
    for _ in range(parts - 1):
        hi = x.astype(BF16)
        out.append(hi)
        x = x - hi.astype(F32)
    out.append(x.astype(BF16))
    return out


def _cumsum_rows(tri, x, parts):
    return sum(jnp.dot(tri, part, preferred_element_type=F32) for part in _split_bf16(x, parts))


def _head_sum(x, ones_bd):
    return jnp.dot(x.astype(BF16), ones_bd, preferred_element_type=F32)


def _rmsnorm(x, g):
    ms = jnp.mean(x * x, axis=-1, keepdims=True)
    return x * lax.rsqrt(ms + RMS_EPS) * g


def _sigmoid(x):
    return 0.5 * jnp.tanh(0.5 * x) + 0.5


def _silu(x):
    return x * _sigmoid(x)


def _iota2(shape, axis):
    return lax.broadcasted_iota(jnp.int32, shape, axis)


def _causal_conv(x, prev_ref, w, b, rows):
    xe = jnp.concatenate([prev_ref[...], x], axis=0)
    prev_ref[...] = x[rows - CONV_HALO:, :]
    x1 = pltpu.roll(xe, 1, 0)
    acc = xe * w[3:4, :] + x1 * w[2:3, :] + pltpu.roll(xe * w[1:2, :] + x1 * w[0:1, :], 2, 0)
    return acc[CONV_HALO:, :] + b


def _ffn_body(h_ref, g_ref, wg_ref, wu_ref, wd_ref, fg_ref, o_ref, *, final):
    h = h_ref[...]
    hn = _rmsnorm(h, g_ref[...]).astype(BF16)
    gate = jnp.dot(hn, wg_ref[...], preferred_element_type=F32)
    up = jnp.dot(hn, wu_ref[...], preferred_element_type=F32)
    act = (_silu(gate) * up).astype(BF16)
    out = h + jnp.dot(act, wd_ref[...], preferred_element_type=F32)
    if final:
        out = _rmsnorm(out, fg_ref[...])
    o_ref[...] = out


def _resident(shape):
    nd = len(shape)
    return pl.BlockSpec(shape, lambda *_: (0,) * nd, pipeline_mode=pl.Buffered(1))


def _layer_of_stack(shape, layer):
    return pl.BlockSpec((pl.Squeezed(),) + tuple(shape), lambda *_: (layer, 0, 0), pipeline_mode=pl.Buffered(1))


def _ffn(h2d, g, wg, wu, wd, fg, layer, final):
    t = h2d.shape[0]
    rows = min(FFN_ROWS, t)
    row_spec = pl.BlockSpec((rows, D_MODEL), lambda i: (i, 0))
    return pl.pallas_call(
        functools.partial(_ffn_body, final=final),
        grid=(t // rows,),
        in_specs=[row_spec, _resident((1, D_MODEL)), _layer_of_stack((D_MODEL, D_FF), layer),
                  _layer_of_stack((D_MODEL, D_FF), layer), _layer_of_stack((D_FF, D_MODEL), layer),
                  _resident((1, D_MODEL))],
        out_specs=row_spec,
        out_shape=jax.ShapeDtypeStruct((t, D_MODEL), F32),
        compiler_params=pltpu.CompilerParams(dimension_semantics=("arbitrary",), vmem_limit_bytes=VMEM_LIMIT),
        name="ffn_final" if final else "ffn",
    )(h2d, g, wg, wu, wd, fg)


def _pool_mixer(u, prev_ref, poolw_bd, pscale, first_pos, rows):
    ue = jnp.concatenate([prev_ref[...], u], axis=0)
    prev_ref[...] = u[rows - POOL_HALO:, :]
    sums = []
    acc = ue
    shift = 1
    for _ in POOL_WINDOWS:
        acc = acc + pltpu.roll(acc, shift, 0)
        sums.append(acc[POOL_HALO:, :])
        shift *= 2
    pos = (first_pos + 1 + _iota2((rows, 1), 0)).astype(F32)
    lane = _iota2((1, POOL_WIDTH), 1)
    pooled = sums[-1] * (1.0 / jnp.minimum(pos, float(POOL_WINDOWS[-1])))
    for gi in range(len(POOL_WINDOWS) - 2, -1, -1):
        cand = sums[gi] * (1.0 / jnp.minimum(pos, float(POOL_WINDOWS[gi])))
        pooled = jnp.where(lane < (gi + 1) * POOL_GROUP_DIM, cand, pooled)
    return _mm(pooled - u, poolw_bd) * pscale


def _ssd_chunk(c, refs, consts):
    xc_s, da_s, b_s, c_s, y_s, state_ref = refs
    tri, causal, lane_lo = consts
    L = SSD_CHUNK
    rows = pl.ds(c * L, L)
    cum = _cumsum_rows(tri, da_s[rows, :], 2)
    xc = xc_s[rows, :]
    bmat = b_s[rows, :]
    cmat = c_s[rows, :]
    last = cum[L - 1:L, :]
    ecum = jnp.exp(cum)
    to_end = jnp.exp(last - cum)
    chunk_decay = jnp.exp(last)
    for g in range(SSD_GROUPS):
        gs = slice(g * SSD_GROUP_WIDTH, (g + 1) * SSD_GROUP_WIDTH)
        ns = slice(g * SSD_STATE, (g + 1) * SSD_STATE)
        bg, cg = bmat[:, ns], cmat[:, ns]
        cb = jnp.where(causal, _mm_nt(cg, bg), 0.0)
        hprev = state_ref[g]
        y_g = _mm(cg, hprev) * ecum[:, gs]
        ypairs = []
        for k in range(SSD_GROUP_WIDTH // 128):
            ps = slice(g * SSD_GROUP_WIDTH + k * 128, g * SSD_GROUP_WIDTH + (k + 1) * 128)
            slab = cum[:, ps]
            swapped = pltpu.roll(slab, 64, 1)
            xpair = xc[:, ps]
            outs = []
            for col in (jnp.where(lane_lo, slab, swapped), jnp.where(lane_lo, swapped, slab)):
                seg = col - col.T
                outs.append(_mm(cb * jnp.exp(jnp.minimum(seg, 0.0)), xpair))
            ypairs.append(jnp.where(lane_lo, outs[0], outs[1]))
        y_s[rows, gs] = y_g + jnp.concatenate(ypairs, axis=1)
        state_ref[g] = hprev * chunk_decay[:, gs] + _mm_tn(bg, xc[:, gs] * to_end[:, gs])
    return 0


def _even_body(h_ref, ng_ref, win_ref, wout_ref, poolw_ref, pscale_ref, convw_ref, convb_ref, dtb_ref, alog_ref,
               dskip_ref, ssdg_ref, expand_ref, o_ref, upool_prev, xbc_prev, state_ref, xc_s, da_s, b_s, c_s, y_s, *,
               rows):
    s = pl.program_id(1)

    @pl.when(s == 0)
    def _():
        upool_prev[...] = jnp.zeros_like(upool_prev)
        xbc_prev[...] = jnp.zeros_like(xbc_prev)
        state_ref[...] = jnp.zeros_like(state_ref)

    o1, o2, o3 = POOL_WIDTH, POOL_WIDTH + SSD_WIDTH, POOL_WIDTH + SSD_WIDTH + SSD_CONV_DIM
    sub = min(SUB_ROWS, rows)
    subs = [slice(j * sub, (j + 1) * sub) for j in range(rows // sub)]

    L = SSD_CHUNK
    r_i, c_i = _iota2((L, L), 0), _iota2((L, L), 1)
    consts = ((r_i >= c_i).astype(BF16), r_i >= c_i, _iota2((1, 128), 1) < 64)
    refs = (xc_s, da_s, b_s, c_s, y_s, state_ref)
    wout = wout_ref[...]

    def front(j, rs):
        p = _mm(_rmsnorm(h_ref[rs, :], ng_ref[...]), win_ref[...])
        y_pool = _pool_mixer(p[:, :o1], upool_prev, poolw_ref[...], pscale_ref[...], s * rows + j * sub, sub)
        xbc = _silu(_causal_conv(p[:, o2:o3], xbc_prev, convw_ref[...], convb_ref[...], sub))
        xh = xbc[:, :SSD_WIDTH]
        dt_heads = jax.nn.softplus(p[:, o3:] + dtb_ref[...])
        dt = sum(jnp.dot(part, expand_ref[...], preferred_element_type=F32) for part in _split_bf16(dt_heads, 3))
        xc_s[rs, :] = xh * dt
        da_s[rs, :] = dt * (-jnp.exp(alog_ref[...]))
        b_s[rs, :] = xbc[:, SSD_WIDTH:SSD_WIDTH + SSD_GROUPS * SSD_STATE]
        c_s[rs, :] = xbc[:, SSD_WIDTH + SSD_GROUPS * SSD_STATE:]
        return y_pool, xh, p[:, o1:o2]

    def scan(j):
        for c in range(j * (sub // L), (j + 1) * (sub // L)):
            _ssd_chunk(c, refs, consts)

    def tail(rs, y_pool, xh, z):
        y = (y_s[rs, :] + dskip_ref[...] * xh) * _silu(z)
        normed = []
        for g in range(SSD_GROUPS):
            yy = y[:, g * SSD_GROUP_WIDTH:(g + 1) * SSD_GROUP_WIDTH]
            normed.append(yy * lax.rsqrt(jnp.mean(yy * yy, axis=-1, keepdims=True) + RMS_EPS))
        y_ssd = jnp.concatenate(normed, axis=1) * ssdg_ref[...]
        o_ref[rs, :] = h_ref[rs, :] + _mm(y_pool, wout[:POOL_WIDTH, :]) + _mm(y_ssd, wout[POOL_WIDTH:, :])

    fronts = [front(j, rs) for j, rs in enumerate(subs)]
    for j in range(len(subs)):
        scan(j)
    for rs, parts in zip(subs, fronts):
        tail(rs, *parts)


def _seq_call(body, name, h, consts, scratch, rows):
    bsz, seq, _ = h.shape
    tile = pl.BlockSpec((pl.Squeezed(), rows, D_MODEL), lambda b, s: (b, s, 0))
    return pl.pallas_call(
        functools.partial(body, rows=rows),
        grid=(bsz, seq // rows),
        in_specs=[tile] + [_resident(c.shape) for c in consts],
        out_specs=tile,
        out_shape=jax.ShapeDtypeStruct(h.shape, F32),
        scratch_shapes=scratch,
        compiler_params=pltpu.CompilerParams(dimension_semantics=("arbitrary", "arbitrary"),
                                             vmem_limit_bytes=VMEM_LIMIT),
        name=name,
    )(h, *consts)


def _block_diag(w):
    g, c, _ = w.shape
    eye = jnp.eye(g, dtype=w.dtype)
    return (eye[:, None, :, None] * w[:, :, None, :]).reshape(g * c, g * c)


def _row(v):
    return v.reshape(1, -1).astype(F32)


def _even_layer(h, ng, w_in, w_out, pool_w, pool_scale, conv_w, conv_b, dt_bias, a_log, d_skip, norm_g, rows):
    o3 = POOL_WIDTH + SSD_WIDTH + SSD_CONV_DIM
    rep = lambda v: jnp.repeat(v, SSD_HEAD_DIM, axis=-1)
    w_in_p = jnp.concatenate([w_in[:, :o3], _pad_cols(w_in[:, o3:], SSD_DT_PAD)], axis=1).astype(BF16)
    head_of_channel = jnp.arange(SSD_WIDTH) // SSD_HEAD_DIM
    expand = (jnp.arange(SSD_DT_PAD)[:, None] == head_of_channel[None, :]).astype(BF16)
    consts = [_row(ng), w_in_p, w_out.astype(BF16), _block_diag(pool_w).astype(BF16), _row(pool_scale),
              conv_w.astype(F32), _row(conv_b), _pad_cols(_row(dt_bias), SSD_DT_PAD), _row(rep(a_log)),
              _row(rep(d_skip)), _row(norm_g), expand]
    scratch = [pltpu.VMEM((POOL_HALO, POOL_WIDTH), F32), pltpu.VMEM((CONV_HALO, SSD_CONV_DIM), F32),
               pltpu.VMEM((SSD_GROUPS, SSD_STATE, SSD_GROUP_WIDTH), F32),
               pltpu.VMEM((rows, SSD_WIDTH), F32), pltpu.VMEM((rows, SSD_WIDTH), F32),
               pltpu.VMEM((rows, SSD_GROUPS * SSD_STATE), F32), pltpu.VMEM((rows, SSD_GROUPS * SSD_STATE), F32),
               pltpu.VMEM((rows, SSD_WIDTH), F32)]
    return _seq_call(_even_body, "even_mixer", h, consts, scratch, rows)


def _stack_heads(x, lane_lo):
    return jnp.concatenate([jnp.where(lane_lo, x, 0.0), jnp.where(lane_lo, 0.0, x)], axis=0)


def _chunk_dot(a, b, ca, cb):
    return lax.dot_general(a.astype(BF16), b.astype(BF16), (((ca,), (cb,)), ((), ())), preferred_element_type=F32)


def _rwkv_prepare_chunk(c, refs, consts):
    r_s, k_s, v_s, kk_s, a_s, ld_s, rhat_s, yv_s, mt_s, gt_s, dec_s = refs
    tri, strict, incl, lane_lo, level_masks, eye = consts
    L = RWKV_CHUNK
    ops = []
    for j in range(RWKV_PREP_CHUNKS):
        cj = c * RWKV_PREP_CHUNKS + j
        rows = pl.ds(pl.multiple_of(cj * L, L), L)
        ld = ld_s[rows, :]
        cum = _cumsum_rows(tri, ld, 3)
        last = cum[L - 1:L, :]
        e_neg, e_end = jnp.exp(-cum), jnp.exp(last - cum)
        kk = kk_s[rows, :]
        kka = kk * a_s[rows, :]
        k = k_s[rows, :]
        ops.append(dict(a=-kk * jnp.exp(cum - ld), r=r_s[rows, :] * jnp.exp(cum), b=kka * e_neg, k=k * e_neg,
                        be=kka * e_end, ke=k * e_end, v=v_s[rows, :]))
        dec_s[pl.ds(pl.multiple_of(cj * 8, 8), 8), :] = jnp.broadcast_to(jnp.exp(last), (8, RWKV_WIDTH))

    items = [(j, q) for j in range(RWKV_PREP_CHUNKS) for q in range(RWKV_PAIRS)]
    P = range(len(items))
    st = lambda name, i: _stack_heads(ops[items[i][0]][name][:, items[i][1] * 128:(items[i][1] + 1) * 128], lane_lo)
    a2 = [st("a", i) for i in P]
    r2 = [st("r", i) for i in P]
    bk2 = [jnp.concatenate([st("b", i), st("k", i)], axis=0) for i in P]
    v2 = [st("v", i) for i in P]
    n_a = [jnp.where(strict, _chunk_dot(a2[i], bk2[i], 1, 1), 0.0) for i in P]
    n_r = [jnp.where(incl, _chunk_dot(r2[i], bk2[i], 1, 1), 0.0) for i in P]
    n_ab = [n_a[i][:, :2 * L] for i in P]

    x = [eye + jnp.where(level_masks[0], n_ab[i], 0.0) for i in P]
    for m in level_masks[1:]:
        t = [_chunk_dot(jnp.where(m, n_ab[i], 0.0), x[i], 1, 0) for i in P]
        x = [x[i] + _chunk_dot(x[i], t[i], 1, 0) for i in P]

    akv = [_chunk_dot(n_a[i][:, 2 * L:], v2[i], 1, 0) for i in P]
    au = [_chunk_dot(x[i], jnp.concatenate([a2[i], akv[i]], axis=1), 1, 0) for i in P]
    ru = [_chunk_dot(n_r[i][:, :2 * L], au[i], 1, 0) for i in P]
    rkv = [_chunk_dot(n_r[i][:, 2 * L:], v2[i], 1, 0) for i in P]
    mg = [_chunk_dot(au[i], st("be", i), 0, 0) for i in P]
    vk = [_chunk_dot(v2[i], st("ke", i), 0, 0) for i in P]
    for i in P:
        idx = (c * RWKV_PREP_CHUNKS + items[i][0]) * RWKV_PAIRS + items[i][1]
        rhat_s[idx] = (r2[i] + ru[i][:, :2 * L]).astype(BF16)
        yv = ru[i][:, 2 * L:] + rkv[i]
        yv_s[idx] = yv[:L, :] + yv[L:, :]
        mt_s[idx] = mg[i][:2 * L, :].astype(BF16)
        gt_s[idx] = mg[i][2 * L:, :] + vk[i]
    return 0


def _rwkv_scan_chunk(c, refs):
    rhat_s, yv_s, mt_s, gt_s, dec_s, y_s, state_ref = refs
    L = RWKV_CHUNK
    rows = pl.ds(pl.multiple_of(c * L, L), L)
    decay = dec_s[pl.ds(pl.multiple_of(c * 8, 8), 8), :][:1, :]
    for q in range(RWKV_PAIRS):
        idx = c * RWKV_PAIRS + q
        sl = slice(q * 128, (q + 1) * 128)
        state = state_ref[q]
        y2 = _chunk_dot(rhat_s[idx], state, 1, 1)
        y_s[rows, sl] = y2[:L, :] + y2[L:, :] + yv_s[idx]
        state_ref[q] = state * decay[:, sl] + _chunk_dot(state, mt_s[idx], 1, 0) + gt_s[idx]
    return 0


def _lru_scan(a, b, carry_ref, rows):
    groups, width = rows // 8, a.shape[1]
    a3, b3 = a.reshape(groups, 8, width), b.reshape(groups, 8, width)
    sub = _iota2((1, 8, 1), 1)
    d = 1
    while d < 8:
        valid = sub >= d
        b3 = jnp.where(valid, a3 * pltpu.roll(b3, d, 1) + b3, b3)
        a3 = jnp.where(valid, a3 * pltpu.roll(a3, d, 1), a3)
        d *= 2
    h_in = carry_ref[CONV_HALO - 1:CONV_HALO, :]
    blocks = []
    for g in range(groups):
        blocks.append(b3[g] + a3[g] * h_in)
        h_in = blocks[-1][7:8, :]
    carry_ref[...] = blocks[-1]
    return jnp.concatenate(blocks, axis=0)


def _odd_body(h_ref, ng_ref, win_ref, wout_ref, mu_ref, w0_ref, wup_ref, a0_ref, aup_ref, gup_ref, kk_ref, ka_ref,
              rk_ref, lng_ref, lnb_ref, lcw_ref, lcb_ref, wax_ref, bax_ref, lam_ref, ones_ref, o_ref,
              shift_prev, lru_prev, lru_carry, state_ref, r_s, k_s, v_s, kk_s, a_s, ld_s, y_s,
              rhat_s, yv_s, mt_s, gt_s, dec_s, *, rows):
    s = pl.program_id(1)

    @pl.when(s == 0)
    def _():
        shift_prev[...] = jnp.zeros_like(shift_prev)
        lru_prev[...] = jnp.zeros_like(lru_prev)
        lru_carry[...] = jnp.zeros_like(lru_carry)
        state_ref[...] = jnp.zeros_like(state_ref)

    W = RWKV_WIDTH
    ones_bd = ones_ref[...]
    sub = rows
    subs = [slice(j * sub, (j + 1) * sub) for j in range(rows // sub)]

    def front(rs):
        p = _mm(_rmsnorm(h_ref[rs, :], ng_ref[...]), win_ref[...])
        pr = p[:, :RWKV_SHIFT_WIDTH]
        prev = pltpu.roll(jnp.concatenate([shift_prev[...], pr], axis=0), 1, 0)[CONV_HALO:, :]
        shift_prev[...] = pr[sub - CONV_HALO:, :]
        ps = pr + mu_ref[...] * (prev - pr)
        r, k, v = ps[:, :W], ps[:, W:2 * W], ps[:, 2 * W:3 * W]
        o = 3 * W
        wd, ad, gd = (ps[:, o + i * RWKV_LORA_PAD:o + (i + 1) * RWKV_LORA_PAD] for i in range(3))
        log_decay = -math.exp(-RWKV_DECAY_OFFSET) * _sigmoid(w0_ref[...] + _mm(jnp.tanh(wd), wup_ref[...]))
        a = _sigmoid(a0_ref[...] + _mm(ad, aup_ref[...]))
        gate = _mm(_sigmoid(gd), gup_ref[...])
        kk = k * kk_ref[...]
        kk = kk * lax.rsqrt(_head_sum(kk * kk, ones_bd) + 1e-12)
        k = k * (1.0 + (a - 1.0) * ka_ref[...])
        r_s[rs, :], k_s[rs, :], v_s[rs, :], kk_s[rs, :], a_s[rs, :] = r, k, v, kk, a
        ld_s[rs, :] = log_decay

        gate_l = p[:, RWKV_SHIFT_WIDTH:RWKV_SHIFT_WIDTH + LRU_WIDTH]
        xb = _causal_conv(p[:, RWKV_SHIFT_WIDTH + LRU_WIDTH:], lru_prev, lcw_ref[...], lcb_ref[...], sub)
        gates = _sigmoid(_mm(xb, wax_ref[...]) + bax_ref[...])
        log_a = -LRU_C * gates[:, :LRU_WIDTH] * jax.nn.softplus(-lam_ref[...])
        a_l = jnp.exp(log_a)
        b_l = jnp.sqrt(-jnp.tanh(log_a) * (a_l * a_l + 1.0)) * gates[:, LRU_WIDTH:] * xb
        y_lru = _lru_scan(a_l, b_l, lru_carry, sub) * jax.nn.gelu(gate_l)
        return gate, y_lru

    fronts = [front(rs) for rs in subs]

    L = RWKV_CHUNK
    r_i, c_i = _iota2((2 * L, 2 * L), 0), _iota2((2 * L, 2 * L), 1)
    level_masks = []
    b = 1
    while b < L:
        same_block = (r_i & -(2 * b)) == (c_i & -(2 * b))
        level_masks.append(same_block & ((r_i & b) != 0) & ((c_i & b) == 0))
        b *= 2
    t_i, s_i = _iota2((L, L), 0), _iota2((L, L), 1)
    r_w, c_w = _iota2((2 * L, 4 * L), 0), _iota2((2 * L, 4 * L), 1) & (2 * L - 1)
    consts = ((t_i >= s_i).astype(BF16), r_w > c_w, r_w >= c_w, _iota2((1, 128), 1) < 64, level_masks,
              (r_i == c_i).astype(F32))
    prep_refs = (r_s, k_s, v_s, kk_s, a_s, ld_s, rhat_s, yv_s, mt_s, gt_s, dec_s)
    lax.fori_loop(0, rows // (L * RWKV_PREP_CHUNKS), lambda c, carry: _rwkv_prepare_chunk(c, prep_refs, consts), 0)
    scan_refs = (rhat_s, yv_s, mt_s, gt_s, dec_s, y_s, state_ref)
    lax.fori_loop(0, rows // L, lambda c, carry: _rwkv_scan_chunk(c, scan_refs), 0, unroll=4)

    wout = wout_ref[...]
    inv_n = 1.0 / RWKV_HEAD_DIM
    for rs, (gate, y_lru) in zip(subs, fronts):
        y = y_s[rs, :]
        mean = _head_sum(y, ones_bd) * inv_n
        yc = y - mean
        var = _head_sum(yc * yc, ones_bd) * inv_n
        y = yc * lax.rsqrt(var + RWKV_GN_EPS) * lng_ref[...] + lnb_ref[...]
        y = y + _head_sum(r_s[rs, :] * k_s[rs, :] * rk_ref[...], ones_bd) * v_s[rs, :]
        o_ref[rs, :] = (h_ref[rs, :] + _mm(y * gate, wout[:RWKV_WIDTH, :]) + _mm(y_lru, wout[RWKV_WIDTH:, :]))


def _pad_rows(w, n):
    return jnp.pad(w, ((0, n - w.shape[0]), (0, 0)))


def _pad_cols(w, n):
    return jnp.pad(w, ((0, 0), (0, n - w.shape[-1])))


def _odd_layer(h, ng, w_in, w_out, mu, w0, w_up, a0, a_up, g_up, k_k, k_a, r_k, ln_g, ln_b, lconv_w, lconv_b,
               wa, ba, wx, bx, lam, rows):
    W, P = RWKV_WIDTH, RWKV_LORA_PAD
    o = 3 * W
    dr, ir = w_up.shape[0], a_up.shape[0]

    def pad_sections(m):
        return jnp.concatenate([m[..., :o], _pad_cols(m[..., o:o + dr], P), _pad_cols(m[..., o + dr:o + dr + ir], P),
                                m[..., o + dr + ir:]], axis=-1)

    w_in_p = pad_sections(w_in).astype(BF16)
    mu_p = pad_sections(mu.reshape(1, -1))
    ones_bd = _block_diag(jnp.ones((W // RWKV_HEAD_DIM, RWKV_HEAD_DIM, RWKV_HEAD_DIM), BF16))
    wax = jnp.concatenate([_block_diag(wa), _block_diag(wx)], axis=1).astype(BF16)
    consts = [_row(ng), w_in_p, w_out.astype(BF16), mu_p.astype(F32), _row(w0), _pad_rows(w_up, P).astype(BF16),
              _row(a0), _pad_rows(a_up, P).astype(BF16), g_up.astype(BF16), _row(k_k), _row(k_a), _row(r_k),
              _row(ln_g), _row(ln_b), lconv_w.astype(F32), _row(lconv_b), wax,
              _row(jnp.concatenate([ba, bx])), _row(lam), ones_bd]
    scratch = [pltpu.VMEM((CONV_HALO, RWKV_SHIFT_WIDTH), F32), pltpu.VMEM((CONV_HALO, LRU_WIDTH), F32),
               pltpu.VMEM((CONV_HALO, LRU_WIDTH), F32), pltpu.VMEM((RWKV_PAIRS, 128, 128), F32)]
    scratch += [pltpu.VMEM((rows, W), F32) for _ in range(7)]
    n_pc = (rows // RWKV_CHUNK) * RWKV_PAIRS
    scratch += [pltpu.VMEM((n_pc, 128, 128), BF16), pltpu.VMEM((n_pc, RWKV_CHUNK, 128), F32),
                pltpu.VMEM((n_pc, 128, 128), BF16), pltpu.VMEM((n_pc, 128, 128), F32),
                pltpu.VMEM((rows // RWKV_CHUNK * 8, W), F32)]
    return _seq_call(_odd_body, "odd_mixer", h, consts, scratch, rows)


def kernel(x, mix_norm_g, ffn_norm_g, ffn_w_gate, ffn_w_up, ffn_w_down, final_norm_g, ev_w_in, ev_w_out, pool_w,
           pool_scale, ssd_conv_w, ssd_conv_b, ssd_dt_bias, ssd_a_log, ssd_d, ssd_norm_g, od_w_in, od_w_out,
           rwkv_mu, rwkv_w0, rwkv_w_up, rwkv_a0, rwkv_a_up, rwkv_g_up, rwkv_k_k, rwkv_k_a, rwkv_r_k, rwkv_ln_g,
           rwkv_ln_b, lru_conv_w, lru_conv_b, lru_wa, lru_ba, lru_wx, lru_bx, lru_lambda):
    bsz, seq, d = x.shape
    assert d == D_MODEL and x.dtype == F32
    rows = min(SEQ_TILE, seq)
    assert seq % rows == 0 and rows % SSD_CHUNK == 0
    depth = mix_norm_g.shape[0]
    w_gate, w_up, w_down = ffn_w_gate.astype(BF16), ffn_w_up.astype(BF16), ffn_w_down.astype(BF16)
    h = x
    for layer in range(depth):
        i = layer // 2
        if layer % 2 == 0:
            h = _even_layer(h, mix_norm_g[layer], ev_w_in[i], ev_w_out[i], pool_w[i], pool_scale[i], ssd_conv_w[i],
                            ssd_conv_b[i], ssd_dt_bias[i], ssd_a_log[i], ssd_d[i], ssd_norm_g[i], rows)
        else:
            h = _odd_layer(h, mix_norm_g[layer], od_w_in[i], od_w_out[i], rwkv_mu[i], rwkv_w0[i], rwkv_w_up[i],
                           rwkv_a0[i], rwkv_a_up[i], rwkv_g_up[i], rwkv_k_k[i], rwkv_k_a[i],
                           rwkv_r_k[i].reshape(-1), rwkv_ln_g[i], rwkv_ln_b[i], lru_conv_w[i], lru_conv_b[i],
                           lru_wa[i], lru_ba[i], lru_wx[i], lru_bx[i], lru_lambda[i], rows)
        final = layer == depth - 1
        h = _ffn(h.reshape(bsz * seq, d), _row(ffn_norm_g[layer]), w_gate, w_up, w_down, _row(final_norm_g),
                 layer, final).reshape(bsz, seq, d)
    return h
```
